```python
import math
import jax
import jax.numpy as jnp
from jax import lax
import numpy as np

D_MODEL = 1024
BATCH = 4
SEQ = 8192
DEPTH = 1

DA_HEADS = 8
DA_HEAD_DIM = 64
DA_V_DIM = 2 * DA_HEAD_DIM
DA_QK_WIDTH = DA_HEADS * 2 * DA_HEAD_DIM
DA_V_WIDTH = DA_HEADS * DA_V_DIM
DA_Q_BLOCK = 128
SSD_D_INNER = 2 * D_MODEL
SSD_HEAD_DIM = 64
SSD_HEADS = SSD_D_INNER // SSD_HEAD_DIM
SSD_GROUPS = 4
SSD_HEADS_PER_GROUP = SSD_HEADS // SSD_GROUPS
SSD_D_STATE = 128
SSD_BC_WIDTH = SSD_GROUPS * SSD_D_STATE
SSD_XBC_WIDTH = SSD_D_INNER + 2 * SSD_BC_WIDTH
SSD_CONV = 4
SSD_CHUNK = 256
DT_MIN = 0.001
DT_MAX = 0.1
N_BRANCHES = 2
IN_SPLITS = (DA_QK_WIDTH, DA_QK_WIDTH, DA_V_WIDTH, SSD_D_INNER, SSD_XBC_WIDTH, SSD_HEADS, N_BRANCHES * D_MODEL)
IN_WIDTH = sum(IN_SPLITS)
PEER_N_KEYS = 128
PEER_N_EXPERTS = PEER_N_KEYS * PEER_N_KEYS
PEER_HEADS = 8
PEER_TOPK = 16
PEER_QUERY_DIM = 256
PEER_HALF_DIM = PEER_QUERY_DIM // 2
PEER_TOKEN_BLOCK = 512
ALPHA = (2 * DEPTH) ** 0.25
BETA = (8 * DEPTH) ** -0.25
EPS = 1e-5

kernel_name = 'hybrid_diffattn_ssd_peer_block'


def layer_norm(x, g, b):
    xf = x.astype(jnp.float32)
    mu = jnp.mean(xf, axis=-1, keepdims=True)
    var = jnp.mean(jnp.square(xf - mu), axis=-1, keepdims=True)
    return ((xf - mu) * lax.rsqrt(var + EPS) * g + b).astype(x.dtype)


def rms_norm(x, w):
    xf = x.astype(jnp.float32)
    return (xf * lax.rsqrt(jnp.mean(jnp.square(xf), axis=-1, keepdims=True) + EPS) * w).astype(x.dtype)


def alibi_slopes(n):
    return jnp.asarray([2.0 ** (-8.0 * (i + 1) / n) for i in range(n)], jnp.float32)


def diff_attention(q, k, v, lam, lambda_init, subln_w):
    bsz, s = q.shape[:2]
    scale = DA_HEAD_DIM ** -0.5
    slopes = alibi_slopes(DA_HEADS)
    outs = []
    for i in range(s // DA_Q_BLOCK):
        q0 = i * DA_Q_BLOCK
        kend = q0 + DA_Q_BLOCK
        scores = jnp.einsum('bqhjd,bkhjd->bhjqk', q[:, q0:kend], k[:, :kend]).astype(jnp.float32) * scale
        dist = (jnp.arange(q0, kend)[:, None] - jnp.arange(kend)[None, :]).astype(jnp.float32)
        bias = jnp.where(dist[None] >= 0, -slopes[:, None, None] * dist[None], -jnp.inf)
        p = jax.nn.softmax(scores + bias[None, :, None], axis=-1)
        a = p[:, :, 0] - lam * p[:, :, 1]
        outs.append(jnp.einsum('bhqk,bkhe->bqhe', a.astype(v.dtype), v[:, :kend]))
    o = jnp.concatenate(outs, axis=1)
    o = rms_norm(o, subln_w) * (1.0 - lambda_init)
    return o.reshape(bsz, s, DA_V_WIDTH)


def causal_depthwise_conv(u, w, b):
    out = lax.conv_general_dilated(u, w[:, None, :].astype(u.dtype), window_strides=(1,),
                                   padding=[(SSD_CONV - 1, 0)], dimension_numbers=('NWC', 'WIO', 'NWC'),
                                   feature_group_count=u.shape[-1])
    return out + b


def ssd_chunked_scan(xh, dt, a, b_in, c_in):
    bsz, s = xh.shape[:2]
    pad = (-s) % SSD_CHUNK
    def padt(t):
        return jnp.pad(t, [(0, 0), (0, pad)] + [(0, 0)] * (t.ndim - 2))
    xh, dt, b_in, c_in = padt(xh), padt(dt), padt(b_in), padt(c_in)
    n_chunks = (s + pad) // SSD_CHUNK
    r = SSD_HEADS_PER_GROUP
    def chunks(t, tail):
        return t.reshape((bsz, n_chunks, SSD_CHUNK) + tail).swapaxes(0, 1)
    xs = (chunks(xh, (SSD_GROUPS, r, SSD_HEAD_DIM)), chunks(dt, (SSD_GROUPS, r)),
          chunks(b_in, (SSD_GROUPS, SSD_D_STATE)), chunks(c_in, (SSD_GROUPS, SSD_D_STATE)))
    a_gr = a.reshape(SSD_GROUPS, r)
    causal = jnp.tril(jnp.ones((SSD_CHUNK, SSD_CHUNK), dtype=bool))[None, :, :, None, None]

    def step(state, inp):
        xc, dtc, bc, cc = inp
        cum = jnp.cumsum(dtc * a_gr, axis=1)
        seg = cum[:, :, None] - cum[:, None, :]
        decay_ts = jnp.exp(jnp.where(causal, seg, -jnp.inf))
        cb = jnp.einsum('btgn,bsgn->btsg', cc, bc)
        w = cb[..., None] * decay_ts * dtc[:, None]
        y = jnp.einsum('btsgr,bsgrp->btgrp', w, xc)
        y = y + jnp.einsum('btgn,bgrpn->btgrp', cc, state) * jnp.exp(cum)[..., None]
        decay_end = jnp.exp(cum[:, -1:] - cum) * dtc
        state = state * jnp.exp(cum[:, -1])[..., None, None] + jnp.einsum('bsgn,bsgrp->bgrpn', bc, decay_end[..., None] * xc)
        return state, y

    state0 = jnp.zeros((bsz, SSD_GROUPS, r, SSD_HEAD_DIM, SSD_D_STATE), jnp.float32)
    _, ys = lax.scan(step, state0, xs)
    ys = ys.swapaxes(0, 1).reshape(bsz, n_chunks * SSD_CHUNK, SSD_HEADS, SSD_HEAD_DIM)
    return ys[:, :s]


def mamba2_branch(z, xbc, dt_raw, conv_w, conv_b, dt_bias, a_log, d_skip, norm_w):
    bsz, s = z.shape[:2]
    f32 = jnp.float32
    xbc = jax.nn.silu(causal_depthwise_conv(xbc, conv_w, conv_b))
    xs, b_in, c_in = jnp.split(xbc, [SSD_D_INNER, SSD_D_INNER + SSD_BC_WIDTH], axis=-1)
    xh = xs.reshape(bsz, s, SSD_HEADS, SSD_HEAD_DIM).astype(f32)
    dt = jax.nn.softplus(dt_raw.astype(f32) + dt_bias)
    a = -jnp.exp(a_log.astype(f32))
    y = ssd_chunked_scan(xh, dt, a,
                         b_in.reshape(bsz, s, SSD_GROUPS, SSD_D_STATE).astype(f32),
                         c_in.reshape(bsz, s, SSD_GROUPS, SSD_D_STATE).astype(f32))
    y = y + d_skip.astype(f32)[:, None] * xh
    y = y.reshape(bsz, s, SSD_D_INNER) * jax.nn.silu(z.astype(f32))
    y = rms_norm(y.reshape(bsz, s, SSD_GROUPS, SSD_D_INNER // SSD_GROUPS),
                 norm_w.reshape(SSD_GROUPS, -1)).reshape(bsz, s, SSD_D_INNER)
    return y.astype(z.dtype)


def peer_ffn(h, w_query, sub_keys, expert_u, expert_v):
    bsz, s, d = h.shape
    n_tok = bsz * s
    pad = (-n_tok) % PEER_TOKEN_BLOCK
    blocks = jnp.pad(h.reshape(n_tok, d), ((0, pad), (0, 0))).reshape(-1, PEER_TOKEN_BLOCK, d)

    def one_block(tb):
        q = (tb @ w_query).reshape(-1, PEER_HEADS, 2, PEER_HALF_DIM)
        sc = jnp.einsum('thjd,jkd->thjk', q, sub_keys).astype(jnp.float32)
        val, idx = lax.top_k(sc, PEER_TOPK)
        cand_s = val[:, :, 0, :, None] + val[:, :, 1, None, :]
        cand_i = idx[:, :, 0, :, None] * PEER_N_KEYS + idx[:, :, 1, None, :]
        cand_s = cand_s.reshape(cand_s.shape[0], PEER_HEADS, PEER_TOPK * PEER_TOPK)
        cand_i = cand_i.reshape(cand_i.shape[0], PEER_HEADS, PEER_TOPK * PEER_TOPK)
        top_s, pos = lax.top_k(cand_s, PEER_TOPK)
        eidx = jnp.take_along_axis(cand_i, pos, axis=-1)
        g = jax.nn.softmax(top_s, axis=-1)
        act = jax.nn.gelu(jnp.einsum('thkd,td->thk', expert_u[eidx], tb), approximate=False)
        wts = (g * act.astype(jnp.float32)).astype(tb.dtype)
        return jnp.einsum('thk,thkd->td', wts, expert_v[eidx])

    out = lax.map(one_block, blocks)
    return out.reshape(-1, d)[:n_tok].reshape(bsz, s, d)


def setup_inputs(seed: int = 0) -> dict:
    key = jax.random.key(seed)
    ks = jax.random.split(key, 28)
    f32 = jnp.float32
    L = DEPTH
    def nrm(k, shape, std):
        return jax.random.normal(k, shape, f32) * std
    dt0 = jnp.exp(jax.random.uniform(ks[7], (L, SSD_HEADS), f32) * (math.log(DT_MAX) - math.log(DT_MIN)) + math.log(DT_MIN))
    return {
        'x': nrm(ks[0], (BATCH, SEQ, D_MODEL), 1.0),
        'c': nrm(ks[1], (BATCH, D_MODEL), 1.0),
        'w_ada': nrm(ks[2], (L, D_MODEL, 6 * D_MODEL), 0.5 * D_MODEL ** -0.5),
        'b_ada': nrm(ks[3], (L, 6 * D_MODEL), 0.02),
        'w_in': nrm(ks[4], (L, D_MODEL, IN_WIDTH), D_MODEL ** -0.5),
        'conv_w': nrm(ks[5], (L, SSD_CONV, SSD_XBC_WIDTH), SSD_CONV ** -0.5),
        'conv_b': nrm(ks[6], (L, SSD_XBC_WIDTH), 0.02),
        'dt_bias': dt0 + jnp.log(-jnp.expm1(-dt0)),
        'a_log': jnp.log(jax.random.uniform(ks[8], (L, SSD_HEADS), f32, 1.0, 16.0)),
        'd_skip': 1.0 + nrm(ks[9], (L, SSD_HEADS), 0.1),
        'ssd_norm_w': 1.0 + nrm(ks[10], (L, SSD_D_INNER), 0.1),
        'lambda_q1': nrm(ks[11], (L, DA_HEAD_DIM), 0.1),
        'lambda_k1': nrm(ks[12], (L, DA_HEAD_DIM), 0.1),
        'lambda_q2': nrm(ks[13], (L, DA_HEAD_DIM), 0.1),
        'lambda_k2': nrm(ks[14], (L, DA_HEAD_DIM), 0.1),
        'da_subln_w': 1.0 + nrm(ks[15], (L, DA_V_DIM), 0.1),
        'w_attn_branch': nrm(ks[16], (L, DA_V_WIDTH, D_MODEL), DA_V_WIDTH ** -0.5),
        'w_ssd_branch': nrm(ks[17], (L, SSD_D_INNER, D_MODEL), SSD_D_INNER ** -0.5),
        'w_out': nrm(ks[18], (L, D_MODEL, D_MODEL), BETA * D_MODEL ** -0.5),
        'ln1_g': 1.0 + nrm(ks[19], (L, D_MODEL), 0.1),
        'ln1_b': nrm(ks[20], (L, D_MODEL), 0.02),
        'peer_w_query': nrm(ks[21], (L, D_MODEL, PEER_HEADS * PEER_QUERY_DIM), D_MODEL ** -0.5),
        'peer_sub_keys': nrm(ks[22], (L, 2, PEER_N_KEYS, PEER_HALF_DIM), PEER_HALF_DIM ** -0.5),
        'peer_u': nrm(ks[23], (L, PEER_N_EXPERTS, D_MODEL), D_MODEL ** -0.5),
        'peer_v': nrm(ks[24], (L, PEER_N_EXPERTS, D_MODEL), BETA * PEER_HEADS ** -0.5),
        'ln2_g': 1.0 + nrm(ks[25], (L, D_MODEL), 0.1),
        'ln2_b': nrm(ks[26], (L, D_MODEL), 0.02),
    }


def reference(x, c, w_ada, b_ada, w_in, conv_w, conv_b, dt_bias, a_log, d_skip, ssd_norm_w,
              lambda_q1, lambda_k1, lambda_q2, lambda_k2, da_subln_w, w_attn_branch, w_ssd_branch,
              w_out, ln1_g, ln1_b, peer_w_query, peer_sub_keys, peer_u, peer_v, ln2_g, ln2_b):
    bsz, s, _ = x.shape
    split_at = np.cumsum(IN_SPLITS)[:-1].tolist()
    f32 = jnp.float32
    for l in range(DEPTH):
        mod = jax.nn.silu(c) @ w_ada[l] + b_ada[l]
        shift1, scale1, gate1, shift2, scale2, gate2 = jnp.split(mod[:, None, :], 6, axis=-1)
        h = x * (1.0 + scale1) + shift1
        q, k, v, z, xbc, dt_raw, g_logits = jnp.split(h @ w_in[l], split_at, axis=-1)
        lambda_init = 0.8 - 0.6 * math.exp(-0.3 * l)
        lam = (jnp.exp(jnp.sum(lambda_q1[l].astype(f32) * lambda_k1[l].astype(f32)))
               - jnp.exp(jnp.sum(lambda_q2[l].astype(f32) * lambda_k2[l].astype(f32))) + lambda_init)
        y_attn = diff_attention(q.reshape(bsz, s, DA_HEADS, 2, DA_HEAD_DIM),
                                k.reshape(bsz, s, DA_HEADS, 2, DA_HEAD_DIM),
                                v.reshape(bsz, s, DA_HEADS, DA_V_DIM),
                                lam, lambda_init, da_subln_w[l]) @ w_attn_branch[l]
        y_ssd = mamba2_branch(z, xbc, dt_raw, conv_w[l], conv_b[l], dt_bias[l], a_log[l],
                              d_skip[l], ssd_norm_w[l]) @ w_ssd_branch[l]
        g_attn, g_ssd = jnp.split(jax.nn.sigmoid(g_logits), N_BRANCHES, axis=-1)
        mixed = (g_attn * y_attn + g_ssd * y_ssd) @ w_out[l]
        x = layer_norm(ALPHA * x + gate1 * mixed, ln1_g[l], ln1_b[l])
        h2 = x * (1.0 + scale2) + shift2
        y_ffn = peer_ffn(h2, peer_w_query[l], peer_sub_keys[l], peer_u[l], peer_v[l])
        x = layer_norm(ALPHA * x + gate2 * y_ffn, ln2_g[l], ln2_b[l])
    return x
```

```python
import functools
import math

import jax
import jax.numpy as jnp
from jax import lax
from jax.experimental import pallas as pl
from jax.experimental.pallas import tpu as pltpu

F32 = jnp.float32
BF16 = jnp.bfloat16

DA_HEADS = 8
DA_HEAD_DIM = 64
DA_V_DIM = 2 * DA_HEAD_DIM
SSD_HEAD_DIM = 64
SSD_GROUPS = 4
SSD_D_STATE = 128
SSD_CONV = 4
SSD_CHUNK = 256
PEER_N_KEYS = 128
PEER_HEADS = 8
PEER_TOPK = 16
PEER_HALF_DIM = 128
DEPTH = 1
ALPHA = (2 * DEPTH) ** 0.25
EPS = 1e-5
LOG2E = 1.4426950408889634
NEG_BIG = -1e30

VMEM_LIMIT_BYTES = 56 * 1024 * 1024


def _cparams(sem):
    return pltpu.CompilerParams(dimension_semantics=sem, vmem_limit_bytes=VMEM_LIMIT_BYTES)


def _dot(a, b):
    return jnp.dot(a, b, preferred_element_type=F32)


def _dot_nt(a, b):
    return lax.dot_general(a, b, (((1,), (1,)), ((), ())), preferred_element_type=F32)


def _ada_kernel(c_ref, w_ref, b_ref, o_ref):
    c = c_ref[...]
    sc = c * jax.nn.sigmoid(c)
    o_ref[...] = jnp.dot(sc, w_ref[...], preferred_element_type=F32,
                         precision=lax.Precision.HIGHEST) + b_ref[...]


def ada_mod(c_pad, w_ada, b_ada):
    d, n = w_ada.shape
    tn = 512
    return pl.pallas_call(
        _ada_kernel,
        grid=(n // tn,),
        in_specs=[pl.BlockSpec((c_pad.shape[0], d), lambda j: (0, 0)),
                  pl.BlockSpec((d, tn), lambda j: (0, j)),
                  pl.BlockSpec((1, tn), lambda j: (0, j))],
        out_specs=pl.BlockSpec((c_pad.shape[0], tn), lambda j: (0, j)),
        out_shape=jax.ShapeDtypeStruct((c_pad.shape[0], n), F32),
        compiler_params=_cparams(("arbitrary",)),
        name="ada_mod",
    )(c_pad, w_ada, b_ada.reshape(1, n))


def _inproj_kernel(x_ref, sh_ref, sc_ref, w_ref, wdt_ref, o_ref, dt_ref, h_ref):
    @pl.when(pl.program_id(1) == 0)
    def _():
        h = x_ref[...] * (1.0 + sc_ref[0]) + sh_ref[0]
        hb = h.astype(BF16)
        h_ref[...] = hb
        dt_ref[...] = _dot(hb, wdt_ref[...])

    o_ref[...] = _dot(h_ref[...], w_ref[...]).astype(BF16)


def in_proj(x2d, shift1, scale1, w_main, w_dt, seq):
    n, d = x2d.shape
    tm, tn = 1024, 1024
    tiles_per_batch = seq // tm
    width = w_main.shape[1]
    bmap = lambda i, j: (i // tiles_per_batch, 0, 0)
    return pl.pallas_call(
        _inproj_kernel,
        grid=(n // tm, width // tn),
        in_specs=[pl.BlockSpec((tm, d), lambda i, j: (i, 0)),
                  pl.BlockSpec((1, 1, d), bmap),
                  pl.BlockSpec((1, 1, d), bmap),
                  pl.BlockSpec((d, tn), lambda i, j: (0, j)),
                  pl.BlockSpec((d, 128), lambda i, j: (0, 0))],
        out_specs=[pl.BlockSpec((tm, tn), lambda i, j: (i, j)),
                   pl.BlockSpec((tm, 128), lambda i, j: (i, 0))],
        out_shape=[jax.ShapeDtypeStruct((n, width), BF16),
                   jax.ShapeDtypeStruct((n, 128), F32)],
        scratch_shapes=[pltpu.VMEM((tm, d), BF16)],
        compiler_params=_cparams(("parallel", "arbitrary")),
        name="in_proj",
    )(x2d, shift1, scale1, w_main, w_dt)


ATT_TQ = 256
ATT_TK = 512


def _attn_kernel(q_ref, k_ref, v_ref, lam_ref, w_ref, o_ref,
                 vt_ref, m_ref, l_ref, acc_ref, *, seq, lambda_init):
    tq, tk = ATT_TQ, ATT_TK
    h = pl.program_id(1)
    qi = pl.program_id(2)
    n_kb_total = seq // tk

    @pl.when(qi == 0)
    def _():
        for c in range(n_kb_total):
            blk = v_ref[0, c * tk:(c + 1) * tk, :].astype(F32)
            vt_ref[c] = blk.T.astype(BF16)

    slope2 = jnp.exp2(-(h + 1).astype(F32)) * LOG2E
    c2 = (DA_HEAD_DIM ** -0.5) * LOG2E
    i0 = qi * tq
    q = q_ref[0]
    rows = lax.broadcasted_iota(jnp.int32, (tk, tq), 0)
    cols = lax.broadcasted_iota(jnp.int32, (tk, tq), 1)
    bias0 = rows.astype(F32) * slope2

    m_ref[...] = jnp.full(m_ref.shape, NEG_BIG, F32)
    l_ref[...] = jnp.zeros(l_ref.shape, F32)
    acc_ref[...] = jnp.zeros(acc_ref.shape, F32)

    def step(kb, masked):
        j0 = kb * tk
        kblk = k_ref[0, pl.ds(pl.multiple_of(j0, tk), tk), :]
        vt = vt_ref[kb]
        cb = slope2 * (j0 - i0).astype(F32)
        if masked:
            keep = (rows + j0) <= (cols + i0)
        for c in range(2):
            kc = kblk[:, c * DA_HEAD_DIM:(c + 1) * DA_HEAD_DIM]
            qc = q[:, c * DA_HEAD_DIM:(c + 1) * DA_HEAD_DIM]
            s = _dot_nt(kc, qc) * c2 + bias0
            if masked:
                s = jnp.where(keep, s, NEG_BIG)
            m_old = m_ref[c]
            m_new = jnp.maximum(m_old, jnp.max(s, axis=0, keepdims=True) + cb)
            p = jnp.exp2(s - (m_new - cb))
            alpha = jnp.exp2(m_old - m_new)
            l_ref[c] = alpha * l_ref[c] + jnp.sum(p, axis=0, keepdims=True)
            acc_ref[c] = alpha * acc_ref[c] + _dot(vt, p.astype(BF16))
            m_ref[c] = m_new

    kb_diag = i0 // tk

    def body(kb, carry):
        step(kb, False)
        return carry

    lax.fori_loop(0, kb_diag, body, 0)
    step(kb_diag, True)

    lam_p = lam_ref[...]
    lam = (jnp.exp(jnp.sum(lam_p[0:1] * lam_p[1:2])) - jnp.exp(jnp.sum(lam_p[2:3] * lam_p[3:4]))
           + lambda_init)
    o_t = acc_ref[0] / l_ref[0] - lam * (acc_ref[1] / l_ref[1])
    o = o_t.T
    o = o * lax.rsqrt(jnp.mean(o * o, axis=-1, keepdims=True) + EPS) * w_ref[...]
    o_ref[0] = (o * (1.0 - lambda_init)).astype(BF16)


def diff_attention(proj3, lam_params, subln_w, q_col0, k_col0, v_col0, lambda_init):
    b, s, _ = proj3.shape
    tq, tk = ATT_TQ, ATT_TK
    qb, kb, vb = q_col0 // 128, k_col0 // 128, v_col0 // 128
    kern = functools.partial(_attn_kernel, seq=s, lambda_init=lambda_init)
    return pl.pallas_call(
        kern,
        grid=(b, DA_HEADS, s // tq),
        in_specs=[pl.BlockSpec((1, tq, 128), lambda bi, h, qi: (bi, qi, qb + h)),
                  pl.BlockSpec((1, s, 128), lambda bi, h, qi: (bi, 0, kb + h)),
                  pl.BlockSpec((1, s, 128), lambda bi, h, qi: (bi, 0, vb + h)),
                  pl.BlockSpec((4, DA_HEAD_DIM), lambda bi, h, qi: (0, 0)),
                  pl.BlockSpec((1, DA_V_DIM), lambda bi, h, qi: (0, 0))],
        out_specs=pl.BlockSpec((1, tq, 128), lambda bi, h, qi: (bi, qi, h)),
        out_shape=jax.ShapeDtypeStruct((b, s, DA_HEADS * DA_V_DIM), BF16),
        scratch_shapes=[pltpu.VMEM((s // tk, DA_V_DIM, tk), BF16),
                        pltpu.VMEM((2, 1, tq), F32),
                        pltpu.VMEM((2, 1, tq), F32),
                        pltpu.VMEM((2, DA_V_DIM, tq), F32)],
        compiler_params=_cparams(("parallel", "arbitrary", "arbitrary")),
        name="diff_attn",
    )(proj3, proj3, proj3, lam_params, subln_w)


def _split3(x):
    hi = x.astype(BF16)
    r1 = x - hi.astype(F32)
    mid = r1.astype(BF16)
    lo = (r1 - mid.astype(F32)).astype(BF16)
    return hi, mid, lo


def _ssd_kernel(xbc_ref, z_ref, dt_ref, cw_ref, cb_ref, dtb_ref, alog_ref, dsk_ref, nw_ref,
                o_ref, ext_ref, state_ref, y_ref, *, d_inner, n_heads):
    L = SSD_CHUNK
    P = SSD_HEAD_DIM
    NS = SSD_D_STATE
    G = SSD_GROUPS
    R = n_heads // G
    bc_w = G * NS
    ci = pl.program_id(1)

    @pl.when(ci == 0)
    def _():
        state_ref[...] = jnp.zeros(state_ref.shape, F32)
        ext_ref[0:8, :] = jnp.zeros((8, ext_ref.shape[1]), F32)

    ext_ref[8:8 + L, :] = xbc_ref[0].astype(F32)
    conv = cb_ref[...]
    for k in range(SSD_CONV):
        off = 8 - (SSD_CONV - 1) + k
        conv = conv + cw_ref[k:k + 1, :] * ext_ref[off:off + L, :]
    ext_ref[0:8, :] = ext_ref[L:L + 8, :]
    xc = conv * jax.nn.sigmoid(conv)

    dtr = dt_ref[0] + dtb_ref[...]
    dt = jnp.maximum(dtr, 0.0) + jnp.log1p(jnp.exp(-jnp.abs(dtr)))
    a = -jnp.exp(alog_ref[...])
    da = dt * a
    tri = (lax.broadcasted_iota(jnp.int32, (L, L), 0) >= lax.broadcasted_iota(jnp.int32, (L, L), 1))
    tri_b = tri.astype(BF16)
    hi, mid, lo = _split3(da)
    cum = _dot(tri_b, hi) + _dot(tri_b, mid) + _dot(tri_b, lo)
    cum_t = cum.T
    dt_t = dt.T
    cum_last = cum[L - 1:L, :]
    e_cum = jnp.exp(cum)
    dec_end = jnp.exp(cum_last - cum) * dt
    e_last = jnp.exp(cum_last)

    for g in range(G):
        bg = xc[:, d_inner + g * NS:d_inner + (g + 1) * NS]
        cg = xc[:, d_inner + bc_w + g * NS:d_inner + bc_w + (g + 1) * NS]
        bg_b = bg.astype(BF16)
        cg_b = cg.astype(BF16)
        bgt_b = bg.T.astype(BF16)
        cbm = _dot_nt(cg_b, bg_b)
        st_g = state_ref[g]
        y_state = _dot(cg_b, st_g.astype(BF16))
        for r in range(R):
            hh = g * R + r
            xh = xc[:, hh * P:(hh + 1) * P]
            seg = cum[:, hh:hh + 1] - cum_t[hh:hh + 1, :]
            w = jnp.where(tri, jnp.exp(seg), 0.0) * cbm * dt_t[hh:hh + 1, :]
            yh = _dot(w.astype(BF16), xh.astype(BF16))
            yh = yh + y_state[:, r * P:(r + 1) * P] * e_cum[:, hh:hh + 1]
            y_ref[:, hh * P:(hh + 1) * P] = yh
            dx = (xh * dec_end[:, hh:hh + 1]).astype(BF16)
            state_ref[g, :, r * P:(r + 1) * P] = (
                st_g[:, r * P:(r + 1) * P] * e_last[:, hh:hh + 1] + _dot(bgt_b, dx))

    xs = xc[:, :d_inner]
    z = z_ref[0].astype(F32)
    y = (y_ref[...] + dsk_ref[...] * xs) * (z * jax.nn.sigmoid(z))
    gw = d_inner // G
    for g in range(G):
        yg = y[:, g * gw:(g + 1) * gw]
        yn = yg * lax.rsqrt(jnp.mean(yg * yg, axis=-1, keepdims=True) + EPS)
        o_ref[0, :, g * gw:(g + 1) * gw] = (yn * nw_ref[:, g * gw:(g + 1) * gw]).astype(BF16)


def ssd_branch(proj3, dt3, conv_w, conv_b, dt_bias_p, a_log_p, dskip_e, norm_w,
               xbc_col0, z_col0, d_inner, n_heads):
    b, s, _ = proj3.shape
    L = SSD_CHUNK
    xw = conv_w.shape[1]
    kern = functools.partial(_ssd_kernel, d_inner=d_inner, n_heads=n_heads)
    cst = lambda bi, ci: (0, 0)
    return pl.pallas_call(
        kern,
        grid=(b, s // L),
        in_specs=[pl.BlockSpec((1, L, xw), lambda bi, ci: (bi, ci, xbc_col0 // xw)),
                  pl.BlockSpec((1, L, d_inner), lambda bi, ci: (bi, ci, z_col0 // d_inner)),
                  pl.BlockSpec((1, L, 128), lambda bi, ci: (bi, ci, 0)),
                  pl.BlockSpec((SSD_CONV, xw), cst),
                  pl.BlockSpec((1, xw), cst),
                  pl.BlockSpec((1, 128), cst),
                  pl.BlockSpec((1, 128), cst),
                  pl.BlockSpec((1, d_inner), cst),
                  pl.BlockSpec((1, d_inner), cst)],
        out_specs=pl.BlockSpec((1, L, d_inner), lambda bi, ci: (bi, ci, 0)),
        out_shape=jax.ShapeDtypeStruct((b, s, d_inner), BF16),
        scratch_shapes=[pltpu.VMEM((L + 8, xw), F32),
                        pltpu.VMEM((SSD_GROUPS, SSD_D_STATE, d_inner // SSD_GROUPS), F32),
                        pltpu.VMEM((L, d_inner), F32)],
        compiler_params=_cparams(("parallel", "arbitrary")),
        name="ssd_scan",
    )(proj3, proj3, dt3, conv_w, conv_b, dt_bias_p, a_log_p, dskip_e, norm_w)


def _layer_norm(v, g, b):
    mu = jnp.mean(v, axis=-1, keepdims=True)
    d = v - mu
    var = jnp.mean(d * d, axis=-1, keepdims=True)
    return d * lax.rsqrt(var + EPS) * g + b


def _merge_kernel(att_ref, ssd_ref, g_ref, x_ref, gate_ref, sh2_ref, sc2_ref,
                  wa_ref, ws_ref, wo_ref, lg_ref, lb_ref, x1_ref, h2_ref, *, d):
    ya = _dot(att_ref[...], wa_ref[...])
    ys = _dot(ssd_ref[...], ws_ref[...])
    gl = jax.nn.sigmoid(g_ref[...].astype(F32))
    mixed = gl[:, :d] * ya + gl[:, d:] * ys
    mo = _dot(mixed.astype(BF16), wo_ref[...])
    x1 = _layer_norm(ALPHA * x_ref[...] + gate_ref[0] * mo, lg_ref[...], lb_ref[...])
    x1_ref[...] = x1
    h2_ref[...] = x1 * (1.0 + sc2_ref[0]) + sh2_ref[0]


def merge_ln1(att2d, ssd2d, proj2d, x2d, gate1, shift2, scale2, w_a, w_s, w_o, ln_g, ln_b,
              g_col0, seq):
    n, d = x2d.shape
    tm = 512
    tiles_per_batch = seq // tm
    bmap = lambda i: (i // tiles_per_batch, 0, 0)
    cst = lambda i: (0, 0)
    kern = functools.partial(_merge_kernel, d=d)
    return pl.pallas_call(
        kern,
        grid=(n // tm,),
        in_specs=[pl.BlockSpec((tm, att2d.shape[1]), lambda i: (i, 0)),
                  pl.BlockSpec((tm, ssd2d.shape[1]), lambda i: (i, 0)),
                  pl.BlockSpec((tm, 2 * d), lambda i: (i, g_col0 // (2 * d))),
                  pl.BlockSpec((tm, d), lambda i: (i, 0)),
                  pl.BlockSpec((1, 1, d), bmap),
                  pl.BlockSpec((1, 1, d), bmap),
                  pl.BlockSpec((1, 1, d), bmap),
                  pl.BlockSpec(w_a.shape, cst),
                  pl.BlockSpec(w_s.shape, cst),
                  pl.BlockSpec(w_o.shape, cst),
                  pl.BlockSpec((1, d), cst),
                  pl.BlockSpec((1, d), cst)],
        out_specs=[pl.BlockSpec((tm, d), lambda i: (i, 0)),
                   pl.BlockSpec((tm, d), lambda i: (i, 0))],
        out_shape=[jax.ShapeDtypeStruct((n, d), F32),
                   jax.ShapeDtypeStruct((n, d), F32)],
        compiler_params=_cparams(("parallel",)),
        name="merge_ln1",
    )(att2d, ssd2d, proj2d, x2d, gate1, shift2, scale2, w_a, w_s, w_o, ln_g, ln_b)


ROUTE_T = 256


def _extract_topk(sc, ids, k):
    n = sc.shape[0]
    vals, idxs = [], []
    for _ in range(k):
        m = jnp.max(sc, axis=0, keepdims=True)
        sel = jnp.min(jnp.where(sc == m, ids, float(1 << 24)), axis=0, keepdims=True)
        hit = ids == sel
        sc = jnp.where(hit, -jnp.inf, sc)
        vals.append(m)
        idxs.append(sel)
    return jnp.concatenate(vals, axis=0), jnp.concatenate(idxs, axis=0), None


def _route_kernel(h2_ref, wq_ref, keys_ref, eidx_ref, gate_ref, q_ref, val_ref, idx_ref,
                  es_ref, gs_ref):
    t = ROUTE_T
    K = PEER_TOPK
    hb = h2_ref[...].astype(BF16)
    q = _dot(hb, wq_ref[...]).astype(BF16)
    for hj in range(2 * PEER_HEADS):
        q_ref[hj] = q[:, hj * PEER_HALF_DIM:(hj + 1) * PEER_HALF_DIM]

    key_ids = lax.broadcasted_iota(jnp.int32, (PEER_N_KEYS, t), 0).astype(F32)

    def level1(hj, carry):
        keys = keys_ref[hj % 2]
        sc = _dot_nt(keys, q_ref[hj])
        vals, idxs, _ = _extract_topk(sc, key_ids, K)
        val_ref[hj] = vals
        idx_ref[hj] = idxs
        return carry

    lax.fori_loop(0, 2 * PEER_HEADS, level1, 0)

    pos_ids = lax.broadcasted_iota(jnp.int32, (K * K, t), 0).astype(F32)

    def level2(h, carry):
        v0, v1 = val_ref[2 * h], val_ref[2 * h + 1]
        i0, i1 = idx_ref[2 * h], idx_ref[2 * h + 1]
        cand_s = jnp.concatenate([v0[a:a + 1, :] + v1 for a in range(K)], axis=0)
        cand_i = jnp.concatenate([i0[a:a + 1, :] * float(PEER_N_KEYS) + i1 for a in range(K)], axis=0)
        sc = cand_s
        tops, exps = [], []
        for _ in range(K):
            m = jnp.max(sc, axis=0, keepdims=True)
            sel = jnp.min(jnp.where(sc == m, pos_ids, float(1 << 24)), axis=0, keepdims=True)
            hit = pos_ids == sel
            exps.append(jnp.max(jnp.where(hit, cand_i, -1.0), axis=0, keepdims=True))
            sc = jnp.where(hit, -jnp.inf, sc)
            tops.append(m)
        top_s = jnp.concatenate(tops, axis=0)
        e = jnp.exp(top_s - top_s[0:1, :])
        gs_ref[h] = e / jnp.sum(e, axis=0, keepdims=True)
        es_ref[h] = jnp.concatenate(exps, axis=0)
        return carry

    lax.fori_loop(0, PEER_HEADS, level2, 0)

    eidx_ref[...] = es_ref[...].reshape(PEER_HEADS * K, t).T.astype(jnp.int32)
    gate_ref[...] = gs_ref[...].reshape(PEER_HEADS * K, t).T


def peer_route(h2, w_query, sub_keys):
    n, d = h2.shape
    t = ROUTE_T
    K = PEER_TOPK
    nsel = PEER_HEADS * K
    cst2 = lambda i: (0, 0)
    return pl.pallas_call(
        _route_kernel,
        grid=(n // t,),
        in_specs=[pl.BlockSpec((t, d), lambda i: (i, 0)),
                  pl.BlockSpec(w_query.shape, cst2),
                  pl.BlockSpec(sub_keys.shape, lambda i: (0, 0, 0))],
        out_specs=[pl.BlockSpec((t, nsel), lambda i: (i, 0)),
                   pl.BlockSpec((t, nsel), lambda i: (i, 0))],
        out_shape=[jax.ShapeDtypeStruct((n, nsel), jnp.int32),
                   jax.ShapeDtypeStruct((n, nsel), F32)],
        scratch_shapes=[pltpu.VMEM((2 * PEER_HEADS, t, PEER_HALF_DIM), BF16),
                        pltpu.VMEM((2 * PEER_HEADS, K, t), F32),
                        pltpu.VMEM((2 * PEER_HEADS, K, t), F32),
                        pltpu.VMEM((PEER_HEADS, K, t), F32),
                        pltpu.VMEM((PEER_HEADS, K, t), F32)],
        compiler_params=_cparams(("parallel",)),
        name="peer_route",
    )(h2, w_query, sub_keys)


from jax.experimental.pallas import tpu_sc as plsc

SC_CORES = 2
SC_SUBCORES = 16
SC_WORKERS = SC_CORES * SC_SUBCORES
SC_LANES = 16
GATHER_ROWS = 32
TOKEN_BLOCK = 8


def _sc_mesh():
    return plsc.VectorSubcoreMesh(core_axis_name="c", subcore_axis_name="s")


def _worker_id():
    return lax.axis_index("s") * SC_CORES + lax.axis_index("c")


def peer_expert_dots(u_tab, h2, eidx):
    n, d = h2.shape
    nsel = eidx.shape[1]
    tok_per_w = n // SC_WORKERS
    n_blocks = tok_per_w // TOKEN_BLOCK
    chunks_per_tok = nsel // GATHER_ROWS
    n_chunks = TOKEN_BLOCK * chunks_per_tok
    n_col = d // SC_LANES
    per_row = 128 // SC_LANES
    rows_per_chunk = GATHER_ROWS // per_row

    @functools.partial(
        pl.kernel, mesh=_sc_mesh(),
        out_type=jax.ShapeDtypeStruct((n * nsel // per_row, 128), F32),
        scratch_types=[pltpu.VMEM((TOKEN_BLOCK, nsel), jnp.int32),
                       pltpu.VMEM((TOKEN_BLOCK, d), F32),
                       pltpu.VMEM((2, GATHER_ROWS, d), F32),
                       pltpu.VMEM((TOKEN_BLOCK * nsel // per_row, 128), F32),
                       pltpu.SemaphoreType.DMA((2,))],
        compiler_params=pltpu.CompilerParams(needs_layout_passes=False),
        name="peer_expert_dots",
    )
    def k(u_hbm, h2_hbm, eidx_hbm, out_hbm, idx_v, x_v, rows_v, o_v, sems):
        wid = _worker_id()

        def gather(q, buf):
            tok = q // chunks_per_tok
            ch = q % chunks_per_tok
            return pltpu.make_async_copy(
                u_hbm.at[idx_v.at[tok, pl.ds(ch * GATHER_ROWS, GATHER_ROWS)]],
                rows_v.at[buf], sems.at[buf])

        def compute(q, buf):
            tok = q // chunks_per_tok

            def cbody(c, accs):
                xc = x_v[tok, pl.ds(c * SC_LANES, SC_LANES)]
                return tuple(accs[r] + rows_v[buf, r, pl.ds(c * SC_LANES, SC_LANES)] * xc
                             for r in range(GATHER_ROWS))

            zero = jnp.zeros((SC_LANES,), F32)
            accs = lax.fori_loop(0, n_col, cbody, tuple(zero for _ in range(GATHER_ROWS)))
            for r in range(GATHER_ROWS):
                o_v[q * rows_per_chunk + r // per_row,
                    pl.ds((r % per_row) * SC_LANES, SC_LANES)] = accs[r]

        def block(bi, carry):
            t0 = wid * tok_per_w + bi * TOKEN_BLOCK
            pltpu.sync_copy(eidx_hbm.at[pl.ds(t0, TOKEN_BLOCK)], idx_v)
            pltpu.sync_copy(h2_hbm.at[pl.ds(t0, TOKEN_BLOCK)], x_v)
            gather(0, 0).start()

            def pair(i, c2):
                q0 = 2 * i
                gather(q0 + 1, 1).start()
                gather(q0, 0).wait()
                compute(q0, 0)

                @pl.when(q0 + 2 < n_chunks)
                def _():
                    gather(q0 + 2, 0).start()

                gather(q0 + 1, 1).wait()
                compute(q0 + 1, 1)
                return c2

            lax.fori_loop(0, n_chunks // 2, pair, 0)
            pltpu.sync_copy(o_v, out_hbm.at[pl.ds(t0 * (nsel // per_row), TOKEN_BLOCK * nsel // per_row)])
            return carry

        lax.fori_loop(0, n_blocks, block, 0)

    return k(u_tab, h2, eidx)


def peer_expert_combine(v_tab, wts, eidx):
    n, nsel = eidx.shape
    d = v_tab.shape[1]
    tok_per_w = n // SC_WORKERS
    n_blocks = tok_per_w // TOKEN_BLOCK
    chunks_per_tok = nsel // GATHER_ROWS
    n_chunks = TOKEN_BLOCK * chunks_per_tok
    n_col = d // SC_LANES

    @functools.partial(
        pl.kernel, mesh=_sc_mesh(),
        out_type=jax.ShapeDtypeStruct((n, d), F32),
        scratch_types=[pltpu.VMEM((TOKEN_BLOCK, nsel), jnp.int32),
                       pltpu.VMEM((TOKEN_BLOCK, nsel), F32),
                       pltpu.VMEM((2, GATHER_ROWS, d), F32),
                       pltpu.VMEM((TOKEN_BLOCK, d), F32),
                       pltpu.SemaphoreType.DMA((2,))],
        compiler_params=pltpu.CompilerParams(needs_layout_passes=False),
        name="peer_expert_combine",
    )
    def k(v_hbm, w_hbm, eidx_hbm, out_hbm, idx_v, w_v, rows_v, y_v, sems):
        wid = _worker_id()

        def gather(q, buf):
            tok = q // chunks_per_tok
            ch = q % chunks_per_tok
            return pltpu.make_async_copy(
                v_hbm.at[idx_v.at[tok, pl.ds(ch * GATHER_ROWS, GATHER_ROWS)]],
                rows_v.at[buf], sems.at[buf])

        def compute(q, buf):
            tok = q // chunks_per_tok
            ch = q % chunks_per_tok
            tok_vec = jnp.full((SC_LANES,), tok, jnp.int32)
            ws = [plsc.load_gather(
                w_v, [tok_vec, jnp.full((SC_LANES,), ch * GATHER_ROWS + r, jnp.int32)])
                for r in range(GATHER_ROWS)]
            first = ch == 0

            def cbody(c, carry):
                sl = pl.ds(c * SC_LANES, SC_LANES)
                acc = jnp.where(first, jnp.zeros((SC_LANES,), F32), y_v[tok, sl])
                for r in range(GATHER_ROWS):
                    acc = acc + ws[r] * rows_v[buf, r, sl]
                y_v[tok, sl] = acc
                return carry

            lax.fori_loop(0, n_col, cbody, 0)

        def block(bi, carry):
            t0 = wid * tok_per_w + bi * TOKEN_BLOCK
            pltpu.sync_copy(eidx_hbm.at[pl.ds(t0, TOKEN_BLOCK)], idx_v)
            pltpu.sync_copy(w_hbm.at[pl.ds(t0, TOKEN_BLOCK)], w_v)
            gather(0, 0).start()

            def pair(i, c2):
                q0 = 2 * i
                gather(q0 + 1, 1).start()
                gather(q0, 0).wait()
                compute(q0, 0)

                @pl.when(q0 + 2 < n_chunks)
                def _():
                    gather(q0 + 2, 0).start()

                gather(q0 + 1, 1).wait()
                compute(q0 + 1, 1)
                return c2

            lax.fori_loop(0, n_chunks // 2, pair, 0)
            pltpu.sync_copy(y_v, out_hbm.at[pl.ds(t0, TOKEN_BLOCK)])
            return carry

        lax.fori_loop(0, n_blocks, block, 0)

    return k(v_tab, wts, eidx)


def _weights_kernel(part_ref, gate_ref, red_ref, o_ref):
    act = jnp.dot(part_ref[...], red_ref[...], preferred_element_type=F32,
                  precision=lax.Precision.HIGHEST)
    gelu = 0.5 * act * (1.0 + lax.erf(act * (2.0 ** -0.5)))
    o_ref[...] = gate_ref[...] * gelu


def peer_weights(part2d, gate):
    n, nsel = gate.shape
    tm = 512
    w = part2d.shape[1]
    red = (jnp.arange(w)[:, None] // SC_LANES == jnp.arange(nsel)[None, :]).astype(F32)
    return pl.pallas_call(
        _weights_kernel,
        grid=(n // tm,),
        in_specs=[pl.BlockSpec((tm, w), lambda i: (i, 0)),
                  pl.BlockSpec((tm, nsel), lambda i: (i, 0)),
                  pl.BlockSpec((w, nsel), lambda i: (0, 0))],
        out_specs=pl.BlockSpec((tm, nsel), lambda i: (i, 0)),
        out_shape=jax.ShapeDtypeStruct((n, nsel), F32),
        compiler_params=_cparams(("parallel",)),
        name="peer_weights",
    )(part2d, gate, red)


def _final_kernel(x1_ref, y_ref, gate_ref, lg_ref, lb_ref, o_ref):
    o_ref[...] = _layer_norm(ALPHA * x1_ref[...] + gate_ref[0] * y_ref[...], lg_ref[...], lb_ref[...])


def final_ln(x1, y_ffn, gate2, ln_g, ln_b, seq):
    n, d = x1.shape
    tm = 1024
    tiles_per_batch = seq // tm
    return pl.pallas_call(
        _final_kernel,
        grid=(n // tm,),
        in_specs=[pl.BlockSpec((tm, d), lambda i: (i, 0)),
                  pl.BlockSpec((tm, d), lambda i: (i, 0)),
                  pl.BlockSpec((1, 1, d), lambda i: (i // tiles_per_batch, 0, 0)),
                  pl.BlockSpec((1, d), lambda i: (0, 0)),
                  pl.BlockSpec((1, d), lambda i: (0, 0))],
        out_specs=pl.BlockSpec((tm, d), lambda i: (i, 0)),
        out_shape=jax.ShapeDtypeStruct((n, d), F32),
        compiler_params=_cparams(("parallel",)),
        name="final_ln",
    )(x1, y_ffn, gate2, ln_g, ln_b)


def _block(x, c, w_ada, b_ada, w_in, conv_w, conv_b, dt_bias, a_log, d_skip, ssd_norm_w,
           lambda_q1, lambda_k1, lambda_q2, lambda_k2, da_subln_w, w_attn_branch, w_ssd_branch,
           w_out, ln1_g, ln1_b, peer_w_query, peer_sub_keys, peer_u, peer_v, ln2_g, ln2_b,
           layer, expert_fn=None):
    bsz, s, d = x.shape
    n = bsz * s
    qk_w = DA_HEADS * 2 * DA_HEAD_DIM
    v_w = DA_HEADS * DA_V_DIM
    d_inner = w_ssd_branch.shape[0]
    n_heads = d_inner // SSD_HEAD_DIM
    bc_w = SSD_GROUPS * SSD_D_STATE
    xbc_w = d_inner + 2 * bc_w

    c_pad = jnp.pad(c, ((0, 8 - bsz), (0, 0)))
    mod = ada_mod(c_pad, w_ada, b_ada)[:bsz]
    shift1, scale1, gate1, shift2, scale2, gate2 = [m.reshape(bsz, 1, d) for m in jnp.split(mod, 6, axis=-1)]

    splits = np_cumsum([qk_w, qk_w, v_w, d_inner, xbc_w, n_heads, 2 * d])
    wq, wk, wv, wz, wxbc, wdt, wg = [w_in[:, a:b] for a, b in zip([0] + splits[:-1], splits)]
    w_main = jnp.concatenate([wxbc, wq, wz, wg, wk, wv], axis=1).astype(BF16)
    w_dt = jnp.pad(wdt, ((0, 0), (0, 128 - n_heads))).astype(BF16)
    xbc_c0 = 0
    q_c0 = xbc_w
    z_c0 = q_c0 + qk_w
    g_c0 = z_c0 + d_inner
    k_c0 = g_c0 + 2 * d
    v_c0 = k_c0 + qk_w
    proj, dt_raw = in_proj(x.reshape(n, d), shift1, scale1, w_main, w_dt, s)
    proj3 = proj.reshape(bsz, s, proj.shape[1])

    lambda_init = 0.8 - 0.6 * math.exp(-0.3 * layer)
    lam_params = jnp.stack([lambda_q1, lambda_k1, lambda_q2, lambda_k2]).astype(F32)
    y_attn = diff_attention(proj3, lam_params, da_subln_w.reshape(1, DA_V_DIM), q_c0, k_c0, v_c0,
                            lambda_init)

    pad_h = lambda t: jnp.pad(t.astype(F32), (0, 128 - n_heads)).reshape(1, 128)
    y_ssd = ssd_branch(proj3, dt_raw.reshape(bsz, s, 128), conv_w, conv_b.reshape(1, xbc_w),
                       pad_h(dt_bias), pad_h(a_log),
                       jnp.repeat(d_skip.astype(F32), SSD_HEAD_DIM).reshape(1, d_inner),
                       ssd_norm_w.reshape(1, d_inner), xbc_c0, z_c0, d_inner, n_heads)

    x1, h2 = merge_ln1(y_attn.reshape(n, v_w), y_ssd.reshape(n, d_inner), proj, x.reshape(n, d),
                       gate1, shift2, scale2, w_attn_branch.astype(BF16), w_ssd_branch.astype(BF16),
                       w_out.astype(BF16), ln1_g.reshape(1, d), ln1_b.reshape(1, d), g_c0, s)

    eidx, gate = peer_route(h2, peer_w_query.astype(BF16), peer_sub_keys.astype(BF16))
    if expert_fn is None:
        part = peer_expert_dots(peer_u, h2, eidx)
        wts = peer_weights(part.reshape(n, -1), gate)
        y_ffn = peer_expert_combine(peer_v, wts, eidx)
    else:
        y_ffn = expert_fn(peer_u, peer_v, h2, eidx, gate)

    out = final_ln(x1, y_ffn, gate2, ln2_g.reshape(1, d), ln2_b.reshape(1, d), s)
    return out.reshape(bsz, s, d)


def np_cumsum(v):
    out, t = [], 0
    for a in v:
        t += a
        out.append(t)
    return out


def kernel(x, c, w_ada, b_ada, w_in, conv_w, conv_b, dt_bias, a_log, d_skip, ssd_norm_w,
           lambda_q1, lambda_k1, lambda_q2, lambda_k2, da_subln_w, w_attn_branch, w_ssd_branch,
           w_out, ln1_g, ln1_b, peer_w_query, peer_sub_keys, peer_u, peer_v, ln2_g, ln2_b):
    for l in range(w_ada.shape[0]):
        x = _block(x, c, w_ada[l], b_ada[l], w_in[l], conv_w[l], conv_b[l], dt_bias[l], a_log[l],
                   d_skip[l], ssd_norm_w[l], lambda_q1[l], lambda_k1[l], lambda_q2[l], lambda_k2[l],
                   da_subln_w[l], w_attn_branch[l], w_ssd_branch[l], w_out[l], ln1_g[l], ln1_b[l],
                   peer_w_query[l], peer_sub_keys[l], peer_u[l], peer_v[l], ln2_g[l], ln2_b[l], l)
    return x
```

```python
import functools
import math

import jax
import jax.numpy as jnp
from jax import lax
from jax.experimental import pallas as pl
from jax.experimental.pallas import tpu as pltpu
from jax.experimental.pallas import tpu_sc as plsc

F32 = jnp.float32
BF16 = jnp.bfloat16

DA_HEADS = 8
DA_HEAD_DIM = 64
DA_V_DIM = 2 * DA_HEAD_DIM
SSD_HEAD_DIM = 64
SSD_GROUPS = 4
SSD_D_STATE = 128
SSD_CONV = 4
SSD_CHUNK = 256
PEER_N_KEYS = 128
PEER_HEADS = 8
PEER_TOPK = 16
PEER_HALF_DIM = 128
DEPTH = 1
ALPHA = (2 * DEPTH) ** 0.25
EPS = 1e-5
LOG2E = 1.4426950408889634
NEG_BIG = -1e30

VMEM_LIMIT_BYTES = 56 * 1024 * 1024


def _cparams(sem):
    return pltpu.CompilerParams(dimension_semantics=sem, vmem_limit_bytes=VMEM_LIMIT_BYTES)


def _dot(a, b):
    return jnp.dot(a, b, preferred_element_type=F32)


def _dot_nt(a, b):
    return lax.dot_general(a, b, (((1,), (1,)), ((), ())), preferred_element_type=F32)


def _ada_kernel(c_ref, w_ref, b_ref, o_ref):
    c = c_ref[...]
    sc = c * jax.nn.sigmoid(c)
    o_ref[...] = jnp.dot(sc, w_ref[...], preferred_element_type=F32,
                         precision=lax.Precision.HIGHEST) + b_ref[...]


def ada_mod(c_pad, w_ada, b_ada):
    d, n = w_ada.shape
    tn = 512
    return pl.pallas_call(
        _ada_kernel,
        grid=(n // tn,),
        in_specs=[pl.BlockSpec((c_pad.shape[0], d), lambda j: (0, 0)),
                  pl.BlockSpec((d, tn), lambda j: (0, j)),
                  pl.BlockSpec((1, tn), lambda j: (0, j))],
        out_specs=pl.BlockSpec((c_pad.shape[0], tn), lambda j: (0, j)),
        out_shape=jax.ShapeDtypeStruct((c_pad.shape[0], n), F32),
        compiler_params=_cparams(("arbitrary",)),
        name="ada_mod",
    )(c_pad, w_ada, b_ada.reshape(1, n))


def _inproj_kernel(x_ref, sh_ref, sc_ref, w_ref, wdt_ref, o_ref, dt_ref, h_ref):
    @pl.when(pl.program_id(1) == 0)
    def _():
        h = x_ref[...] * (1.0 + sc_ref[0]) + sh_ref[0]
        hb = h.astype(BF16)
        h_ref[...] = hb
        dt_ref[...] = _dot(hb, wdt_ref[...])

    o_ref[...] = _dot(h_ref[...], w_ref[...]).astype(BF16)


def in_proj(x2d, shift1, scale1, w_main, w_dt, seq):
    n, d = x2d.shape
    tm, tn = 1024, 1024
    tiles_per_batch = seq // tm
    width = w_main.shape[1]
    bmap = lambda i, j: (i // tiles_per_batch, 0, 0)
    return pl.pallas_call(
        _inproj_kernel,
        grid=(n // tm, width // tn),
        in_specs=[pl.BlockSpec((tm, d), lambda i, j: (i, 0)),
                  pl.BlockSpec((1, 1, d), bmap),
                  pl.BlockSpec((1, 1, d), bmap),
                  pl.BlockSpec((d, tn), lambda i, j: (0, j)),
                  pl.BlockSpec((d, 128), lambda i, j: (0, 0))],
        out_specs=[pl.BlockSpec((tm, tn), lambda i, j: (i, j)),
                   pl.BlockSpec((tm, 128), lambda i, j: (i, 0))],
        out_shape=[jax.ShapeDtypeStruct((n, width), BF16),
                   jax.ShapeDtypeStruct((n, 128), F32)],
        scratch_shapes=[pltpu.VMEM((tm, d), BF16)],
        compiler_params=_cparams(("parallel", "arbitrary")),
        name="in_proj",
    )(x2d, shift1, scale1, w_main, w_dt)


ATT_TQ = 256
ATT_TK = 512


def _attn_kernel(q_ref, k_ref, v_ref, lam_ref, w_ref, o_ref,
                 vt_ref, m_ref, l_ref, acc_ref, *, seq, lambda_init):
    tq, tk = ATT_TQ, ATT_TK
    h = pl.program_id(1)
    qi = pl.program_id(2)
    n_kb_total = seq // tk

    @pl.when(qi == 0)
    def _():
        for c in range(n_kb_total):
            blk = v_ref[0, c * tk:(c + 1) * tk, :].astype(F32)
            vt_ref[c] = blk.T.astype(BF16)

    slope2 = jnp.exp2(-(h + 1).astype(F32)) * LOG2E
    c2 = (DA_HEAD_DIM ** -0.5) * LOG2E
    i0 = qi * tq
    q = q_ref[0]
    rows = lax.broadcasted_iota(jnp.int32, (tk, tq), 0)
    cols = lax.broadcasted_iota(jnp.int32, (tk, tq), 1)
    bias0 = rows.astype(F32) * slope2

    m_ref[...] = jnp.full(m_ref.shape, NEG_BIG, F32)
    l_ref[...] = jnp.zeros(l_ref.shape, F32)
    acc_ref[...] = jnp.zeros(acc_ref.shape, F32)

    def step(kb, masked):
        j0 = kb * tk
        kblk = k_ref[0, pl.ds(pl.multiple_of(j0, tk), tk), :]
        vt = vt_ref[kb]
        cb = slope2 * (j0 - i0).astype(F32)
        if masked:
            keep = (rows + j0) <= (cols + i0)
        for c in range(2):
            kc = kblk[:, c * DA_HEAD_DIM:(c + 1) * DA_HEAD_DIM]
            qc = q[:, c * DA_HEAD_DIM:(c + 1) * DA_HEAD_DIM]
            s = _dot_nt(kc, qc) * c2 + bias0
            if masked:
                s = jnp.where(keep, s, NEG_BIG)
            m_old = m_ref[c]
            m_new = jnp.maximum(m_old, jnp.max(s, axis=0, keepdims=True) + cb)
            p = jnp.exp2(s - (m_new - cb))
            alpha = jnp.exp2(m_old - m_new)
            l_ref[c] = alpha * l_ref[c] + jnp.sum(p, axis=0, keepdims=True)
            acc_ref[c] = alpha * acc_ref[c] + _dot(vt, p.astype(BF16))
            m_ref[c] = m_new

    kb_diag = i0 // tk

    def body(kb, carry):
        step(kb, False)
        return carry

    lax.fori_loop(0, kb_diag, body, 0)
    step(kb_diag, True)

    lam_p = lam_ref[...]
    lam = (jnp.exp(jnp.sum(lam_p[0:1] * lam_p[1:2])) - jnp.exp(jnp.sum(lam_p[2:3] * lam_p[3:4]))
           + lambda_init)
    o_t = acc_ref[0] / l_ref[0] - lam * (acc_ref[1] / l_ref[1])
    o = o_t.T
    o = o * lax.rsqrt(jnp.mean(o * o, axis=-1, keepdims=True) + EPS) * w_ref[...]
    o_ref[0] = (o * (1.0 - lambda_init)).astype(BF16)


def diff_attention(proj3, lam_params, subln_w, q_col0, k_col0, v_col0, lambda_init):
    b, s, _ = proj3.shape
    tq, tk = ATT_TQ, ATT_TK
    qb, kb, vb = q_col0 // 128, k_col0 // 128, v_col0 // 128
    kern = functools.partial(_attn_kernel, seq=s, lambda_init=lambda_init)
    return pl.pallas_call(
        kern,
        grid=(b, DA_HEADS, s // tq),
        in_specs=[pl.BlockSpec((1, tq, 128), lambda bi, h, qi: (bi, qi, qb + h)),
                  pl.BlockSpec((1, s, 128), lambda bi, h, qi: (bi, 0, kb + h)),
                  pl.BlockSpec((1, s, 128), lambda bi, h, qi: (bi, 0, vb + h)),
                  pl.BlockSpec((4, DA_HEAD_DIM), lambda bi, h, qi: (0, 0)),
                  pl.BlockSpec((1, DA_V_DIM), lambda bi, h, qi: (0, 0))],
        out_specs=pl.BlockSpec((1, tq, 128), lambda bi, h, qi: (bi, qi, h)),
        out_shape=jax.ShapeDtypeStruct((b, s, DA_HEADS * DA_V_DIM), BF16),
        scratch_shapes=[pltpu.VMEM((s // tk, DA_V_DIM, tk), BF16),
                        pltpu.VMEM((2, 1, tq), F32),
                        pltpu.VMEM((2, 1, tq), F32),
                        pltpu.VMEM((2, DA_V_DIM, tq), F32)],
        compiler_params=_cparams(("parallel", "arbitrary", "arbitrary")),
        name="diff_attn",
    )(proj3, proj3, proj3, lam_params, subln_w)


def _split3(x):
    hi = x.astype(BF16)
    r1 = x - hi.astype(F32)
    mid = r1.astype(BF16)
    lo = (r1 - mid.astype(F32)).astype(BF16)
    return hi, mid, lo


def _ssd_kernel(xbc_ref, z_ref, dt_ref, cw_ref, cb_ref, dtb_ref, alog_ref, dsk_ref, nw_ref,
                o_ref, ext_ref, state_ref, y_ref, *, d_inner, n_heads):
    L = SSD_CHUNK
    P = SSD_HEAD_DIM
    NS = SSD_D_STATE
    G = SSD_GROUPS
    R = n_heads // G
    bc_w = G * NS
    ci = pl.program_id(1)

    @pl.when(ci == 0)
    def _():
        state_ref[...] = jnp.zeros(state_ref.shape, F32)
        ext_ref[0:8, :] = jnp.zeros((8, ext_ref.shape[1]), F32)

    ext_ref[8:8 + L, :] = xbc_ref[0].astype(F32)
    conv = cb_ref[...]
    for k in range(SSD_CONV):
        off = 8 - (SSD_CONV - 1) + k
        conv = conv + cw_ref[k:k + 1, :] * ext_ref[off:off + L, :]
    ext_ref[0:8, :] = ext_ref[L:L + 8, :]
    xc = conv * jax.nn.sigmoid(conv)

    dtr = dt_ref[0] + dtb_ref[...]
    dt = jnp.maximum(dtr, 0.0) + jnp.log1p(jnp.exp(-jnp.abs(dtr)))
    a = -jnp.exp(alog_ref[...])
    da = dt * a
    tri = (lax.broadcasted_iota(jnp.int32, (L, L), 0) >= lax.broadcasted_iota(jnp.int32, (L, L), 1))
    tri_b = tri.astype(BF16)
    hi, mid, lo = _split3(da)
    cum = _dot(tri_b, hi) + _dot(tri_b, mid) + _dot(tri_b, lo)
    cum_t = cum.T
    dt_t = dt.T
    cum_last = cum[L - 1:L, :]
    e_cum = jnp.exp(cum)
    dec_end = jnp.exp(cum_last - cum) * dt
    e_last = jnp.exp(cum_last)

    for g in range(G):
        bg = xc[:, d_inner + g * NS:d_inner + (g + 1) * NS]
        cg = xc[:, d_inner + bc_w + g * NS:d_inner + bc_w + (g + 1) * NS]
        bg_b = bg.astype(BF16)
        cg_b = cg.astype(BF16)
        bgt_b = bg.T.astype(BF16)
        cbm = _dot_nt(cg_b, bg_b)
        st_g = state_ref[g]
        y_state = _dot(cg_b, st_g.astype(BF16))
        for r in range(R):
            hh = g * R + r
            xh = xc[:, hh * P:(hh + 1) * P]
            seg = cum[:, hh:hh + 1] - cum_t[hh:hh + 1, :]
            w = jnp.where(tri, jnp.exp(seg), 0.0) * cbm * dt_t[hh:hh + 1, :]
            yh = _dot(w.astype(BF16), xh.astype(BF16))
            yh = yh + y_state[:, r * P:(r + 1) * P] * e_cum[:, hh:hh + 1]
            y_ref[:, hh * P:(hh + 1) * P] = yh
            dx = (xh * dec_end[:, hh:hh + 1]).astype(BF16)
            state_ref[g, :, r * P:(r + 1) * P] = (
                st_g[:, r * P:(r + 1) * P] * e_last[:, hh:hh + 1] + _dot(bgt_b, dx))

    xs = xc[:, :d_inner]
    z = z_ref[0].astype(F32)
    y = (y_ref[...] + dsk_ref[...] * xs) * (z * jax.nn.sigmoid(z))
    gw = d_inner // G
    for g in range(G):
        yg = y[:, g * gw:(g + 1) * gw]
        yn = yg * lax.rsqrt(jnp.mean(yg * yg, axis=-1, keepdims=True) + EPS)
        o_ref[0, :, g * gw:(g + 1) * gw] = (yn * nw_ref[:, g * gw:(g + 1) * gw]).astype(BF16)


def ssd_branch(proj3, dt3, conv_w, conv_b, dt_bias_p, a_log_p, dskip_e, norm_w,
               xbc_col0, z_col0, d_inner, n_heads):
    b, s, _ = proj3.shape
    L = SSD_CHUNK
    xw = conv_w.shape[1]
    kern = functools.partial(_ssd_kernel, d_inner=d_inner, n_heads=n_heads)
    cst = lambda bi, ci: (0, 0)
    return pl.pallas_call(
        kern,
        grid=(b, s // L),
        in_specs=[pl.BlockSpec((1, L, xw), lambda bi, ci: (bi, ci, xbc_col0 // xw)),
                  pl.BlockSpec((1, L, d_inner), lambda bi, ci: (bi, ci, z_col0 // d_inner)),
                  pl.BlockSpec((1, L, 128), lambda bi, ci: (bi, ci, 0)),
                  pl.BlockSpec((SSD_CONV, xw), cst),
                  pl.BlockSpec((1, xw), cst),
                  pl.BlockSpec((1, 128), cst),
                  pl.BlockSpec((1, 128), cst),
                  pl.BlockSpec((1, d_inner), cst),
                  pl.BlockSpec((1, d_inner), cst)],
        out_specs=pl.BlockSpec((1, L, d_inner), lambda bi, ci: (bi, ci, 0)),
        out_shape=jax.ShapeDtypeStruct((b, s, d_inner), BF16),
        scratch_shapes=[pltpu.VMEM((L + 8, xw), F32),
                        pltpu.VMEM((SSD_GROUPS, SSD_D_STATE, d_inner // SSD_GROUPS), F32),
                        pltpu.VMEM((L, d_inner), F32)],
        compiler_params=_cparams(("parallel", "arbitrary")),
        name="ssd_scan",
    )(proj3, proj3, dt3, conv_w, conv_b, dt_bias_p, a_log_p, dskip_e, norm_w)


def _layer_norm(v, g, b):
    mu = jnp.mean(v, axis=-1, keepdims=True)
    d = v - mu
    var = jnp.mean(d * d, axis=-1, keepdims=True)
    return d * lax.rsqrt(var + EPS) * g + b


def _merge_kernel(att_ref, ssd_ref, g_ref, x_ref, gate_ref, sh2_ref, sc2_ref,
                  wa_ref, ws_ref, wo_ref, lg_ref, lb_ref, x1_ref, h2_ref, *, d):
    ya = _dot(att_ref[...], wa_ref[...])
    ys = _dot(ssd_ref[...], ws_ref[...])
    gl = jax.nn.sigmoid(g_ref[...].astype(F32))
    mixed = gl[:, :d] * ya + gl[:, d:] * ys
    mo = _dot(mixed.astype(BF16), wo_ref[...])
    x1 = _layer_norm(ALPHA * x_ref[...] + gate_ref[0] * mo, lg_ref[...], lb_ref[...])
    x1_ref[...] = x1
    h2_ref[...] = x1 * (1.0 + sc2_ref[0]) + sh2_ref[0]


def merge_ln1(att2d, ssd2d, proj2d, x2d, gate1, shift2, scale2, w_a, w_s, w_o, ln_g, ln_b,
              g_col0, seq):
    n, d = x2d.shape
    tm = 512
    tiles_per_batch = seq // tm
    bmap = lambda i: (i // tiles_per_batch, 0, 0)
    cst = lambda i: (0, 0)
    kern = functools.partial(_merge_kernel, d=d)
    return pl.pallas_call(
        kern,
        grid=(n // tm,),
        in_specs=[pl.BlockSpec((tm, att2d.shape[1]), lambda i: (i, 0)),
                  pl.BlockSpec((tm, ssd2d.shape[1]), lambda i: (i, 0)),
                  pl.BlockSpec((tm, 2 * d), lambda i: (i, g_col0 // (2 * d))),
                  pl.BlockSpec((tm, d), lambda i: (i, 0)),
                  pl.BlockSpec((1, 1, d), bmap),
                  pl.BlockSpec((1, 1, d), bmap),
                  pl.BlockSpec((1, 1, d), bmap),
                  pl.BlockSpec(w_a.shape, cst),
                  pl.BlockSpec(w_s.shape, cst),
                  pl.BlockSpec(w_o.shape, cst),
                  pl.BlockSpec((1, d), cst),
                  pl.BlockSpec((1, d), cst)],
        out_specs=[pl.BlockSpec((tm, d), lambda i: (i, 0)),
                   pl.BlockSpec((tm, d), lambda i: (i, 0))],
        out_shape=[jax.ShapeDtypeStruct((n, d), F32),
                   jax.ShapeDtypeStruct((n, d), F32)],
        compiler_params=_cparams(("parallel",)),
        name="merge_ln1",
    )(att2d, ssd2d, proj2d, x2d, gate1, shift2, scale2, w_a, w_s, w_o, ln_g, ln_b)


ROUTE_T = 256


def _extract_topk(sc, ids, k):
    vals, idxs = [], []
    for _ in range(k):
        m = jnp.max(sc, axis=0, keepdims=True)
        sel = jnp.min(jnp.where(sc == m, ids, float(1 << 24)), axis=0, keepdims=True)
        hit = ids == sel
        sc = jnp.where(hit, -jnp.inf, sc)
        vals.append(m)
        idxs.append(sel)
    return jnp.concatenate(vals, axis=0), jnp.concatenate(idxs, axis=0)


def _route_kernel(h2_ref, wq_ref, keys_ref, eidx_ref, gate_ref, q_ref, val_ref, idx_ref,
                  es_ref, gs_ref):
    t = ROUTE_T
    K = PEER_TOPK
    hb = h2_ref[...].astype(BF16)
    q = _dot(hb, wq_ref[...]).astype(BF16)
    for hj in range(2 * PEER_HEADS):
        q_ref[hj] = q[:, hj * PEER_HALF_DIM:(hj + 1) * PEER_HALF_DIM]

    key_ids = lax.broadcasted_iota(jnp.int32, (PEER_N_KEYS, t), 0).astype(F32)

    def level1(hj, carry):
        keys = keys_ref[hj % 2]
        sc = _dot_nt(keys, q_ref[hj])
        vals, idxs = _extract_topk(sc, key_ids, K)
        val_ref[hj] = vals
        idx_ref[hj] = idxs
        return carry

    lax.fori_loop(0, 2 * PEER_HEADS, level1, 0)

    pos_ids = lax.broadcasted_iota(jnp.int32, (K * K, t), 0).astype(F32)

    def level2(h, carry):
        v0, v1 = val_ref[2 * h], val_ref[2 * h + 1]
        i0, i1 = idx_ref[2 * h], idx_ref[2 * h + 1]
        cand_s = jnp.concatenate([v0[a:a + 1, :] + v1 for a in range(K)], axis=0)
        cand_i = jnp.concatenate([i0[a:a + 1, :] * float(PEER_N_KEYS) + i1 for a in range(K)], axis=0)
        sc = cand_s
        tops, exps = [], []
        for _ in range(K):
            m = jnp.max(sc, axis=0, keepdims=True)
            sel = jnp.min(jnp.where(sc == m, pos_ids, float(1 << 24)), axis=0, keepdims=True)
            hit = pos_ids == sel
            exps.append(jnp.max(jnp.where(hit, cand_i, -1.0), axis=0, keepdims=True))
            sc = jnp.where(hit, -jnp.inf, sc)
            tops.append(m)
        top_s = jnp.concatenate(tops, axis=0)
        e = jnp.exp(top_s - top_s[0:1, :])
        gs_ref[h] = e / jnp.sum(e, axis=0, keepdims=True)
        es_ref[h] = jnp.concatenate(exps, axis=0)
        return carry

    lax.fori_loop(0, PEER_HEADS, level2, 0)

    eidx_ref[...] = es_ref[...].reshape(PEER_HEADS * K, t).T.astype(jnp.int32)
    gate_ref[...] = gs_ref[...].reshape(PEER_HEADS * K, t).T


def peer_route(h2, w_query, sub_keys):
    n, d = h2.shape
    t = ROUTE_T
    K = PEER_TOPK
    nsel = PEER_HEADS * K
    cst2 = lambda i: (0, 0)
    return pl.pallas_call(
        _route_kernel,
        grid=(n // t,),
        in_specs=[pl.BlockSpec((t, d), lambda i: (i, 0)),
                  pl.BlockSpec(w_query.shape, cst2),
                  pl.BlockSpec(sub_keys.shape, lambda i: (0, 0, 0))],
        out_specs=[pl.BlockSpec((t, nsel), lambda i: (i, 0)),
                   pl.BlockSpec((t, nsel), lambda i: (i, 0))],
        out_shape=[jax.ShapeDtypeStruct((n, nsel), jnp.int32),
                   jax.ShapeDtypeStruct((n, nsel), F32)],
        scratch_shapes=[pltpu.VMEM((2 * PEER_HEADS, t, PEER_HALF_DIM), BF16),
                        pltpu.VMEM((2 * PEER_HEADS, K, t), F32),
                        pltpu.VMEM((2 * PEER_HEADS, K, t), F32),
                        pltpu.VMEM((PEER_HEADS, K, t), F32),
                        pltpu.VMEM((PEER_HEADS, K, t), F32)],
        compiler_params=_cparams(("parallel",)),
        name="peer_route",
    )(h2, w_query, sub_keys)


SC_CORES = 2
SC_SUBCORES = 16
SC_WORKERS = SC_CORES * SC_SUBCORES
SC_LANES = 16
GATHER_ROWS = 32
TOKEN_BLOCK = 8
COMBINE_COLS = 2
DOT_ROWS = 16
PARTS_PER_ROW = 128 // SC_LANES


def pack_table(tab):
    e, d = tab.shape
    bits = lax.bitcast_convert_type(tab.astype(BF16), jnp.uint16).astype(jnp.uint32)
    bits = bits.reshape(e, d // (2 * SC_LANES), 2, SC_LANES)
    packed = bits[:, :, 0, :] | (bits[:, :, 1, :] << 16)
    return lax.bitcast_convert_type(packed, jnp.int32).reshape(e, d // 2)


def _unpack_pair(w):
    lo = lax.bitcast_convert_type(jnp.left_shift(w, 16), F32)
    hi = lax.bitcast_convert_type(jnp.bitwise_and(w, jnp.int32(-65536)), F32)
    return lo, hi


def _sc_mesh():
    return plsc.VectorSubcoreMesh(core_axis_name="c", subcore_axis_name="s")


def _worker_id():
    return lax.axis_index("s") * SC_CORES + lax.axis_index("c")


def _gather_pipeline(tab_hbm, idx_v, rows_v, sems, n_chunks, chunks_per_tok, compute):
    def gather(q, buf):
        tok = q // chunks_per_tok
        ch = q % chunks_per_tok
        return pltpu.make_async_copy(
            tab_hbm.at[idx_v.at[tok, pl.ds(ch * GATHER_ROWS, GATHER_ROWS)]],
            rows_v.at[buf], sems.at[buf])

    gather(0, 0).start()

    def pair(i, carry):
        q0 = 2 * i
        gather(q0 + 1, 1).start()
        gather(q0, 0).wait()
        compute(q0, 0)

        @pl.when(q0 + 2 < n_chunks)
        def _():
            gather(q0 + 2, 0).start()

        gather(q0 + 1, 1).wait()
        compute(q0 + 1, 1)
        return carry

    lax.fori_loop(0, n_chunks // 2, pair, 0)


def _dots_phase(u_hbm, h2_hbm, eidx_hbm, out_hbm, idx_v, x_v, rows_v, o_v, sems, n, d, nsel):
    tok_per_w = n // SC_WORKERS
    chunks_per_tok = nsel // GATHER_ROWS
    n_chunks = TOKEN_BLOCK * chunks_per_tok
    rows_per_chunk = GATHER_ROWS // PARTS_PER_ROW
    out_rows = TOKEN_BLOCK * nsel // PARTS_PER_ROW
    wid = _worker_id()

    def compute(q, buf):
        tok = q // chunks_per_tok
        zero = jnp.zeros((SC_LANES,), F32)
        for r0 in range(0, GATHER_ROWS, DOT_ROWS):
            def cbody(c, accs, r0=r0):
                x_lo = x_v[tok, pl.ds(c * 2 * SC_LANES, SC_LANES)]
                x_hi = x_v[tok, pl.ds(c * 2 * SC_LANES + SC_LANES, SC_LANES)]
                out = []
                for r in range(DOT_ROWS):
                    lo, hi = _unpack_pair(rows_v[buf, r0 + r, pl.ds(c * SC_LANES, SC_LANES)])
                    out.append(accs[r] + lo * x_lo + hi * x_hi)
                return tuple(out)

            accs = lax.fori_loop(0, d // (2 * SC_LANES), cbody, tuple(zero for _ in range(DOT_ROWS)))
            for r in range(DOT_ROWS):
                rr = r0 + r
                o_v[q * rows_per_chunk + rr // PARTS_PER_ROW,
                    pl.ds((rr % PARTS_PER_ROW) * SC_LANES, SC_LANES)] = accs[r]

    def block(bi, carry):
        t0 = wid * tok_per_w + bi * TOKEN_BLOCK
        pltpu.sync_copy(eidx_hbm.at[pl.ds(t0, TOKEN_BLOCK)], idx_v)
        pltpu.sync_copy(h2_hbm.at[pl.ds(t0, TOKEN_BLOCK)], x_v)
        _gather_pipeline(u_hbm, idx_v, rows_v, sems, n_chunks, chunks_per_tok, compute)
        pltpu.sync_copy(o_v, out_hbm.at[pl.ds(t0 * (nsel // PARTS_PER_ROW), out_rows)])
        return carry

    lax.fori_loop(0, tok_per_w // TOKEN_BLOCK, block, 0)


def _combine_phase(v_hbm, w_hbm, eidx_hbm, out_hbm, idx_v, w_v, rows_v, y_v, sems, n, d, nsel):
    tok_per_w = n // SC_WORKERS
    chunks_per_tok = nsel // GATHER_ROWS
    n_chunks = TOKEN_BLOCK * chunks_per_tok
    wid = _worker_id()

    def compute(q, buf):
        tok = q // chunks_per_tok
        ch = q % chunks_per_tok
        tok_vec = jnp.full((SC_LANES,), tok, jnp.int32)
        ws = [plsc.load_gather(
            w_v, [tok_vec, jnp.full((SC_LANES,), ch * GATHER_ROWS + r, jnp.int32)])
            for r in range(GATHER_ROWS)]
        first = ch == 0

        def cbody(cg, carry):
            cols = [cg * COMBINE_COLS + u for u in range(COMBINE_COLS)]
            sl_lo = [pl.ds(c * 2 * SC_LANES, SC_LANES) for c in cols]
            sl_hi = [pl.ds(c * 2 * SC_LANES + SC_LANES, SC_LANES) for c in cols]
            zero = jnp.zeros((SC_LANES,), F32)
            a_lo = [jnp.where(first, zero, y_v[tok, s]) for s in sl_lo]
            a_hi = [jnp.where(first, zero, y_v[tok, s]) for s in sl_hi]
            for r in range(GATHER_ROWS):
                for u in range(COMBINE_COLS):
                    lo, hi = _unpack_pair(rows_v[buf, r, pl.ds(cols[u] * SC_LANES, SC_LANES)])
                    a_lo[u] = a_lo[u] + ws[r] * lo
                    a_hi[u] = a_hi[u] + ws[r] * hi
            for u in range(COMBINE_COLS):
                y_v[tok, sl_lo[u]] = a_lo[u]
                y_v[tok, sl_hi[u]] = a_hi[u]
            return carry

        lax.fori_loop(0, d // (2 * SC_LANES) // COMBINE_COLS, cbody, 0)

    def block(bi, carry):
        t0 = wid * tok_per_w + bi * TOKEN_BLOCK
        pltpu.sync_copy(eidx_hbm.at[pl.ds(t0, TOKEN_BLOCK)], idx_v)
        pltpu.sync_copy(w_hbm.at[pl.ds(t0, TOKEN_BLOCK)], w_v)
        _gather_pipeline(v_hbm, idx_v, rows_v, sems, n_chunks, chunks_per_tok, compute)
        pltpu.sync_copy(y_v, out_hbm.at[pl.ds(t0, TOKEN_BLOCK)])
        return carry

    lax.fori_loop(0, tok_per_w // TOKEN_BLOCK, block, 0)


def peer_experts_sc(u_pk=None, h2=None, eidx=None, v_pk=None, wts_prev=None, eidx_prev=None):
    do_dots = u_pk is not None
    do_comb = v_pk is not None
    ref = h2 if do_dots else wts_prev
    n = ref.shape[0]
    nsel = (eidx if do_dots else eidx_prev).shape[1]
    d = 2 * (u_pk if do_dots else v_pk).shape[1]
    out_types, scratch, args = [], [], []
    if do_comb:
        out_types.append(jax.ShapeDtypeStruct((n, d), F32))
        args += [v_pk, wts_prev, eidx_prev]
        scratch += [pltpu.VMEM((TOKEN_BLOCK, nsel), F32), pltpu.VMEM((TOKEN_BLOCK, d), F32)]
    if do_dots:
        out_types.append(jax.ShapeDtypeStruct((n * nsel // PARTS_PER_ROW, 128), F32))
        args += [u_pk, h2, eidx]
        scratch += [pltpu.VMEM((TOKEN_BLOCK, d), F32),
                    pltpu.VMEM((TOKEN_BLOCK * nsel // PARTS_PER_ROW, 128), F32)]
    scratch += [pltpu.VMEM((TOKEN_BLOCK, nsel), jnp.int32),
                pltpu.VMEM((2, GATHER_ROWS, d // 2), jnp.int32),
                pltpu.SemaphoreType.DMA((2,))]

    def body(*refs):
        refs = list(refs)
        ins = refs[:len(args)]
        outs = refs[len(args):len(args) + len(out_types)]
        scr = refs[len(args) + len(out_types):]
        idx_v, rows_v, sems = scr[-3:]
        if do_comb:
            v_hbm, w_hbm, ep_hbm = ins[:3]
            _combine_phase(v_hbm, w_hbm, ep_hbm, outs[0], idx_v, scr[0], rows_v, scr[1], sems, n, d, nsel)
        if do_dots:
            u_hbm, h2_hbm, e_hbm = ins[-3:]
            x_v, o_v = scr[-5], scr[-4]
            _dots_phase(u_hbm, h2_hbm, e_hbm, outs[-1], idx_v, x_v, rows_v, o_v, sems, n, d, nsel)

    res = pl.kernel(
        body, out_type=tuple(out_types), mesh=_sc_mesh(), scratch_types=scratch,
        compiler_params=pltpu.CompilerParams(needs_layout_passes=False),
        name="peer_experts_sc" + ("_c" if do_comb else "") + ("_d" if do_dots else ""),
    )(*args)
    res = list(res)
    y_prev = res[0] if do_comb else None
    part = res[-1] if do_dots else None
    return y_prev, part


def _weights_kernel(part_ref, gate_ref, red_ref, o_ref):
    act = jnp.dot(part_ref[...], red_ref[...], preferred_element_type=F32,
                  precision=lax.Precision.HIGHEST)
    gelu = 0.5 * act * (1.0 + lax.erf(act * (2.0 ** -0.5)))
    o_ref[...] = gate_ref[...] * gelu


def peer_weights(part2d, gate):
    n, nsel = gate.shape
    tm = 512
    w = part2d.shape[1]
    red = (jnp.arange(w)[:, None] // SC_LANES == jnp.arange(nsel)[None, :]).astype(F32)
    return pl.pallas_call(
        _weights_kernel,
        grid=(n // tm,),
        in_specs=[pl.BlockSpec((tm, w), lambda i: (i, 0)),
                  pl.BlockSpec((tm, nsel), lambda i: (i, 0)),
                  pl.BlockSpec((w, nsel), lambda i: (0, 0))],
        out_specs=pl.BlockSpec((tm, nsel), lambda i: (i, 0)),
        out_shape=jax.ShapeDtypeStruct((n, nsel), F32),
        compiler_params=_cparams(("parallel",)),
        name="peer_weights",
    )(part2d, gate, red)


def _final_kernel(x1_ref, y_ref, gate_ref, lg_ref, lb_ref, o_ref):
    o_ref[...] = _layer_norm(ALPHA * x1_ref[...] + gate_ref[0] * y_ref[...], lg_ref[...], lb_ref[...])


def final_ln(x1, y_ffn, gate2, ln_g, ln_b, seq):
    n, d = x1.shape
    tm = 1024
    tiles_per_batch = seq // tm
    return pl.pallas_call(
        _final_kernel,
        grid=(n // tm,),
        in_specs=[pl.BlockSpec((tm, d), lambda i: (i, 0)),
                  pl.BlockSpec((tm, d), lambda i: (i, 0)),
                  pl.BlockSpec((1, 1, d), lambda i: (i // tiles_per_batch, 0, 0)),
                  pl.BlockSpec((1, d), lambda i: (0, 0)),
                  pl.BlockSpec((1, d), lambda i: (0, 0))],
        out_specs=pl.BlockSpec((tm, d), lambda i: (i, 0)),
        out_shape=jax.ShapeDtypeStruct((n, d), F32),
        compiler_params=_cparams(("parallel",)),
        name="final_ln",
    )(x1, y_ffn, gate2, ln_g, ln_b)


def _cumsum(widths):
    out, t = [], 0
    for w in widths:
        t += w
        out.append(t)
    return out


def _block(x, c, w_ada, b_ada, w_in, conv_w, conv_b, dt_bias, a_log, d_skip, ssd_norm_w,
           lambda_q1, lambda_k1, lambda_q2, lambda_k2, da_subln_w, w_attn_branch, w_ssd_branch,
           w_out, ln1_g, ln1_b, peer_w_query, peer_sub_keys, peer_u, peer_v, ln2_g, ln2_b,
           layer):
    bsz, s, d = x.shape
    qk_w = DA_HEADS * 2 * DA_HEAD_DIM
    v_w = DA_HEADS * DA_V_DIM
    d_inner = w_ssd_branch.shape[0]
    n_heads = d_inner // SSD_HEAD_DIM
    bc_w = SSD_GROUPS * SSD_D_STATE
    xbc_w = d_inner + 2 * bc_w

    c_pad = jnp.pad(c, ((0, 8 - bsz), (0, 0)))
    mod = ada_mod(c_pad, w_ada, b_ada)[:bsz]
    shift1, scale1, gate1, shift2, scale2, gate2 = [m.reshape(bsz, 1, d) for m in jnp.split(mod, 6, axis=-1)]

    splits = _cumsum([qk_w, qk_w, v_w, d_inner, xbc_w, n_heads, 2 * d])
    wq, wk, wv, wz, wxbc, wdt, wg = [w_in[:, a:b] for a, b in zip([0] + splits[:-1], splits)]
    w_main = jnp.concatenate([wxbc, wq, wz, wg, wk, wv], axis=1).astype(BF16)
    w_dt = jnp.pad(wdt, ((0, 0), (0, 128 - n_heads))).astype(BF16)
    xbc_c0 = 0
    q_c0 = xbc_w
    z_c0 = q_c0 + qk_w
    g_c0 = z_c0 + d_inner
    k_c0 = g_c0 + 2 * d
    v_c0 = k_c0 + qk_w
    lambda_init = 0.8 - 0.6 * math.exp(-0.3 * layer)
    lam_params = jnp.stack([lambda_q1, lambda_k1, lambda_q2, lambda_k2]).astype(F32)
    pad_h = lambda t: jnp.pad(t.astype(F32), (0, 128 - n_heads)).reshape(1, 128)
    dtb_p, alog_p = pad_h(dt_bias), pad_h(a_log)
    dskip_e = jnp.repeat(d_skip.astype(F32), SSD_HEAD_DIM).reshape(1, d_inner)
    w_ab, w_sb, w_o = w_attn_branch.astype(BF16), w_ssd_branch.astype(BF16), w_out.astype(BF16)
    w_qp, keys_b = peer_w_query.astype(BF16), peer_sub_keys.astype(BF16)
    u_pk, v_pk = pack_table(peer_u), pack_table(peer_v)

    outs = []
    pending = None
    for b in range(bsz + 1):
        cur = None
        if b < bsz:
            sl = slice(b, b + 1)
            xb = x[b]
            proj, dt_raw = in_proj(xb, shift1[sl], scale1[sl], w_main, w_dt, s)
            proj3 = proj.reshape(1, s, proj.shape[1])
            y_attn = diff_attention(proj3, lam_params, da_subln_w.reshape(1, DA_V_DIM), q_c0, k_c0, v_c0,
                                    lambda_init)
            y_ssd = ssd_branch(proj3, dt_raw.reshape(1, s, 128), conv_w, conv_b.reshape(1, xbc_w),
                               dtb_p, alog_p, dskip_e, ssd_norm_w.reshape(1, d_inner),
                               xbc_c0, z_c0, d_inner, n_heads)
            x1, h2 = merge_ln1(y_attn.reshape(s, v_w), y_ssd.reshape(s, d_inner), proj, xb,
                               gate1[sl], shift2[sl], scale2[sl], w_ab, w_sb, w_o,
                               ln1_g.reshape(1, d), ln1_b.reshape(1, d), g_c0, s)
            eidx, gate = peer_route(h2, w_qp, keys_b)
            cur = (x1, gate2[sl], eidx, gate, h2)
        kw = {}
        if pending is not None:
            p_x1, p_gate2, p_eidx, p_gate, p_part = pending
            wts = peer_weights(p_part.reshape(s, -1), p_gate)
            kw.update(v_pk=v_pk, wts_prev=wts, eidx_prev=p_eidx)
        if cur is not None:
            kw.update(u_pk=u_pk, h2=cur[4], eidx=cur[2])
        y_prev, part = peer_experts_sc(**kw)
        if pending is not None:
            outs.append(final_ln(p_x1, y_prev, p_gate2, ln2_g.reshape(1, d), ln2_b.reshape(1, d), s))
        pending = None if cur is None else (cur[0], cur[1], cur[2], cur[3], part)
    return jnp.stack(outs, axis=0)


def kernel(x, c, w_ada, b_ada, w_in, conv_w, conv_b, dt_bias, a_log, d_skip, ssd_norm_w,
           lambda_q1, lambda_k1, lambda_q2, lambda_k2, da_subln_w, w_attn_branch, w_ssd_branch,
           w_out, ln1_g, ln1_b, peer_w_query, peer_sub_keys, peer_u, peer_v, ln2_g, ln2_b):
    for l in range(w_ada.shape[0]):
        x = _block(x, c, w_ada[l], b_ada[l], w_in[l], conv_w[l], conv_b[l], dt_bias[l], a_log[l],
                   d_skip[l], ssd_norm_w[l], lambda_q1[l], lambda_k1[l], lambda_q2[l], lambda_k2[l],
                   da_subln_w[l], w_attn_branch[l], w_ssd_branch[l], w_out[l], ln1_g[l], ln1_b[l],
                   peer_w_query[l], peer_sub_keys[l], peer_u[l], peer_v[l], ln2_g[l], ln2_b[l], l)
    return x
```

```python
import functools
import math

import jax
import jax.numpy as jnp
from jax import lax
from jax.experimental import pallas as pl
from jax.experimental.pallas import tpu as pltpu
from jax.experimental.pallas import tpu_sc as plsc

F32 = jnp.float32
BF16 = jnp.bfloat16

DA_HEADS = 8
DA_HEAD_DIM = 64
DA_V_DIM = 2 * DA_HEAD_DIM
SSD_HEAD_DIM = 64
SSD_GROUPS = 4
SSD_D_STATE = 128
SSD_CONV = 4
SSD_CHUNK = 256
PEER_N_KEYS = 128
PEER_HEADS = 8
PEER_TOPK = 16
PEER_HALF_DIM = 128
DEPTH = 1
ALPHA = (2 * DEPTH) ** 0.25
EPS = 1e-5
LOG2E = 1.4426950408889634
NEG_BIG = -1e30

VMEM_LIMIT_BYTES = 56 * 1024 * 1024


def _cparams(sem):
    return pltpu.CompilerParams(dimension_semantics=sem, vmem_limit_bytes=VMEM_LIMIT_BYTES)


def _dot(a, b):
    return jnp.dot(a, b, preferred_element_type=F32)


def _dot_nt(a, b):
    return lax.dot_general(a, b, (((1,), (1,)), ((), ())), preferred_element_type=F32)


def _ada_kernel(c_ref, w_ref, b_ref, o_ref):
    c = c_ref[...]
    sc = c * jax.nn.sigmoid(c)
    o_ref[...] = jnp.dot(sc, w_ref[...], preferred_element_type=F32,
                         precision=lax.Precision.HIGHEST) + b_ref[...]


def ada_mod(c_pad, w_ada, b_ada):
    d, n = w_ada.shape
    tn = 512
    return pl.pallas_call(
        _ada_kernel,
        grid=(n // tn,),
        in_specs=[pl.BlockSpec((c_pad.shape[0], d), lambda j: (0, 0)),
                  pl.BlockSpec((d, tn), lambda j: (0, j)),
                  pl.BlockSpec((1, tn), lambda j: (0, j))],
        out_specs=pl.BlockSpec((c_pad.shape[0], tn), lambda j: (0, j)),
        out_shape=jax.ShapeDtypeStruct((c_pad.shape[0], n), F32),
        compiler_params=_cparams(("arbitrary",)),
        name="ada_mod",
    )(c_pad, w_ada, b_ada.reshape(1, n))


def _inproj_kernel(x_ref, sh_ref, sc_ref, w_ref, wdt_ref, o_ref, dt_ref, h_ref):
    @pl.when(pl.program_id(1) == 0)
    def _():
        h = x_ref[...] * (1.0 + sc_ref[0]) + sh_ref[0]
        hb = h.astype(BF16)
        h_ref[...] = hb
        dt_ref[...] = _dot(hb, wdt_ref[...])

    o_ref[...] = _dot(h_ref[...], w_ref[...]).astype(BF16)


def in_proj(x2d, shift1, scale1, w_main, w_dt, seq):
    n, d = x2d.shape
    tm, tn = 1024, 1024
    tiles_per_batch = seq // tm
    width = w_main.shape[1]
    bmap = lambda i, j: (i // tiles_per_batch, 0, 0)
    return pl.pallas_call(
        _inproj_kernel,
        grid=(n // tm, width // tn),
        in_specs=[pl.BlockSpec((tm, d), lambda i, j: (i, 0)),
                  pl.BlockSpec((1, 1, d), bmap),
                  pl.BlockSpec((1, 1, d), bmap),
                  pl.BlockSpec((d, tn), lambda i, j: (0, j)),
                  pl.BlockSpec((d, 128), lambda i, j: (0, 0))],
        out_specs=[pl.BlockSpec((tm, tn), lambda i, j: (i, j)),
                   pl.BlockSpec((tm, 128), lambda i, j: (i, 0))],
        out_shape=[jax.ShapeDtypeStruct((n, width), BF16),
                   jax.ShapeDtypeStruct((n, 128), F32)],
        scratch_shapes=[pltpu.VMEM((tm, d), BF16)],
        compiler_params=_cparams(("parallel", "arbitrary")),
        name="in_proj",
    )(x2d, shift1, scale1, w_main, w_dt)


ATT_TQ = 256
ATT_TK = 512
SEQ_SPLIT = 2


def _attn_kernel(q_ref, k_ref, v_ref, lam_ref, w_ref, o_ref,
                 vt_ref, sa_ref, sb_ref, bias_ref, m_ref, l_ref, acc_ref, *, kv_len, q_block0, lambda_init):
    tq, tk = ATT_TQ, ATT_TK
    h = pl.program_id(1)
    qi = pl.program_id(2) + q_block0
    n_kb_total = kv_len // tk

    @pl.when(pl.program_id(2) == 0)
    def _():
        for c in range(n_kb_total):
            blk = v_ref[0, c * tk:(c + 1) * tk, :].astype(F32)
            vt_ref[c] = blk.T.astype(BF16)

    slope2 = jnp.exp2(-(h + 1).astype(F32)) * LOG2E
    c2 = (DA_HEAD_DIM ** -0.5) * LOG2E
    i0 = qi * tq
    kb_diag = i0 // tk
    q = q_ref[0]
    qs = [q[:, c * DA_HEAD_DIM:(c + 1) * DA_HEAD_DIM] for c in range(2)]
    rows = lax.broadcasted_iota(jnp.int32, (tk, tq), 0)
    cols = lax.broadcasted_iota(jnp.int32, (tk, tq), 1)
    bias0 = rows.astype(F32) * slope2
    bias_ref[0] = bias0
    bias_ref[1] = jnp.where(rows + kb_diag * tk <= cols + i0, bias0, NEG_BIG)

    m_ref[...] = jnp.full(m_ref.shape, NEG_BIG, F32)
    l_ref[...] = jnp.zeros(l_ref.shape, F32)
    acc_ref[...] = jnp.zeros(acc_ref.shape, F32)

    def scores(kb, s_ref):
        kblk = k_ref[0, pl.ds(pl.multiple_of(kb * tk, tk), tk), :]
        for c in range(2):
            s_ref[c] = _dot_nt(kblk[:, c * DA_HEAD_DIM:(c + 1) * DA_HEAD_DIM], qs[c])

    def softmax_pv(kb, s_ref):
        vt = vt_ref[kb]
        cb = slope2 * (kb * tk - i0).astype(F32)
        bias = bias_ref[(kb == kb_diag).astype(jnp.int32)]
        for c in range(2):
            s = s_ref[c] * c2 + bias
            m_old = m_ref[c]
            m_new = jnp.maximum(m_old, jnp.max(s, axis=0, keepdims=True) + cb)
            p = jnp.exp2(s - (m_new - cb))
            alpha = jnp.exp2(m_old - m_new)
            l_ref[c] = alpha * l_ref[c] + jnp.sum(p, axis=0, keepdims=True)
            acc_ref[c] = alpha * acc_ref[c] + _dot(vt, p.astype(BF16))
            m_ref[c] = m_new

    n_blocks = kb_diag + 1
    scores(0, sa_ref)

    def body(i, carry):
        kb = 2 * i
        scores(kb + 1, sb_ref)
        softmax_pv(kb, sa_ref)
        scores(jnp.minimum(kb + 2, kb_diag), sa_ref)
        softmax_pv(kb + 1, sb_ref)
        return carry

    lax.fori_loop(0, n_blocks // 2, body, 0)

    @pl.when(n_blocks % 2 == 1)
    def _():
        softmax_pv(kb_diag, sa_ref)

    lam_p = lam_ref[...]
    lam = (jnp.exp(jnp.sum(lam_p[0:1] * lam_p[1:2])) - jnp.exp(jnp.sum(lam_p[2:3] * lam_p[3:4]))
           + lambda_init)
    o_t = acc_ref[0] / l_ref[0] - lam * (acc_ref[1] / l_ref[1])
    o = o_t.T
    o = o * lax.rsqrt(jnp.mean(o * o, axis=-1, keepdims=True) + EPS) * w_ref[...]
    o_ref[0] = (o * (1.0 - lambda_init)).astype(BF16)


def diff_attention(proj3, lam_params, subln_w, q_col0, k_col0, v_col0, lambda_init, q_block0, n_q_blocks):
    b, s, _ = proj3.shape
    tq, tk = ATT_TQ, ATT_TK
    kv_len = -(-((q_block0 + n_q_blocks) * tq) // tk) * tk
    qb, kb, vb = q_col0 // 128, k_col0 // 128, v_col0 // 128
    kern = functools.partial(_attn_kernel, kv_len=kv_len, q_block0=q_block0, lambda_init=lambda_init)
    return pl.pallas_call(
        kern,
        grid=(b, DA_HEADS, n_q_blocks),
        in_specs=[pl.BlockSpec((1, tq, 128), lambda bi, h, qi: (bi, qi + q_block0, qb + h)),
                  pl.BlockSpec((1, kv_len, 128), lambda bi, h, qi: (bi, 0, kb + h)),
                  pl.BlockSpec((1, kv_len, 128), lambda bi, h, qi: (bi, 0, vb + h)),
                  pl.BlockSpec((4, DA_HEAD_DIM), lambda bi, h, qi: (0, 0)),
                  pl.BlockSpec((1, DA_V_DIM), lambda bi, h, qi: (0, 0))],
        out_specs=pl.BlockSpec((1, tq, 128), lambda bi, h, qi: (bi, qi, h)),
        out_shape=jax.ShapeDtypeStruct((b, n_q_blocks * tq, DA_HEADS * DA_V_DIM), BF16),
        scratch_shapes=[pltpu.VMEM((kv_len // tk, DA_V_DIM, tk), BF16),
                        pltpu.VMEM((2, tk, tq), F32),
                        pltpu.VMEM((2, tk, tq), F32),
                        pltpu.VMEM((2, tk, tq), F32),
                        pltpu.VMEM((2, 1, tq), F32),
                        pltpu.VMEM((2, 1, tq), F32),
                        pltpu.VMEM((2, DA_V_DIM, tq), F32)],
        compiler_params=_cparams(("parallel", "arbitrary", "arbitrary")),
        name="diff_attn",
    )(proj3, proj3, proj3, lam_params, subln_w)


def _split3(x):
    hi = x.astype(BF16)
    r1 = x - hi.astype(F32)
    mid = r1.astype(BF16)
    lo = (r1 - mid.astype(F32)).astype(BF16)
    return hi, mid, lo


def _ssd_kernel(xbc_ref, z_ref, dt_ref, cw_ref, cb_ref, dtb_ref, alog_ref, dsk_ref, nw_ref,
                o_ref, ext_ref, state_ref, y_ref, *, d_inner, n_heads):
    L = SSD_CHUNK
    P = SSD_HEAD_DIM
    NS = SSD_D_STATE
    G = SSD_GROUPS
    R = n_heads // G
    bc_w = G * NS
    ci = pl.program_id(1)

    @pl.when(ci == 0)
    def _():
        state_ref[...] = jnp.zeros(state_ref.shape, F32)
        ext_ref[0:8, :] = jnp.zeros((8, ext_ref.shape[1]), F32)

    ext_ref[8:8 + L, :] = xbc_ref[0].astype(F32)
    conv = cb_ref[...]
    for k in range(SSD_CONV):
        off = 8 - (SSD_CONV - 1) + k
        conv = conv + cw_ref[k:k + 1, :] * ext_ref[off:off + L, :]
    ext_ref[0:8, :] = ext_ref[L:L + 8, :]
    xc = conv * jax.nn.sigmoid(conv)

    dtr = dt_ref[0] + dtb_ref[...]
    dt = jnp.maximum(dtr, 0.0) + jnp.log1p(jnp.exp(-jnp.abs(dtr)))
    a = -jnp.exp(alog_ref[...])
    da = dt * a
    tri = (lax.broadcasted_iota(jnp.int32, (L, L), 0) >= lax.broadcasted_iota(jnp.int32, (L, L), 1))
    tri_b = tri.astype(BF16)
    hi, mid, lo = _split3(da)
    cum = _dot(tri_b, hi) + _dot(tri_b, mid) + _dot(tri_b, lo)
    cum_t = cum.T
    dt_t = dt.T
    cum_last = cum[L - 1:L, :]
    e_cum = jnp.exp(cum)
    dec_end = jnp.exp(cum_last - cum) * dt
    e_last = jnp.exp(cum_last)

    for g in range(G):
        bg = xc[:, d_inner + g * NS:d_inner + (g + 1) * NS]
        cg = xc[:, d_inner + bc_w + g * NS:d_inner + bc_w + (g + 1) * NS]
        bg_b = bg.astype(BF16)
        cg_b = cg.astype(BF16)
        bgt_b = bg.T.astype(BF16)
        cbm = _dot_nt(cg_b, bg_b)
        st_g = state_ref[g]
        y_state = _dot(cg_b, st_g.astype(BF16))
        for r in range(R):
            hh = g * R + r
            xh = xc[:, hh * P:(hh + 1) * P]
            seg = cum[:, hh:hh + 1] - cum_t[hh:hh + 1, :]
            w = jnp.where(tri, jnp.exp(seg), 0.0) * cbm * dt_t[hh:hh + 1, :]
            yh = _dot(w.astype(BF16), xh.astype(BF16))
            yh = yh + y_state[:, r * P:(r + 1) * P] * e_cum[:, hh:hh + 1]
            y_ref[:, hh * P:(hh + 1) * P] = yh
            dx = (xh * dec_end[:, hh:hh + 1]).astype(BF16)
            state_ref[g, :, r * P:(r + 1) * P] = (
                st_g[:, r * P:(r + 1) * P] * e_last[:, hh:hh + 1] + _dot(bgt_b, dx))

    xs = xc[:, :d_inner]
    z = z_ref[0].astype(F32)
    y = (y_ref[...] + dsk_ref[...] * xs) * (z * jax.nn.sigmoid(z))
    gw = d_inner // G
    for g in range(G):
        yg = y[:, g * gw:(g + 1) * gw]
        yn = yg * lax.rsqrt(jnp.mean(yg * yg, axis=-1, keepdims=True) + EPS)
        o_ref[0, :, g * gw:(g + 1) * gw] = (yn * nw_ref[:, g * gw:(g + 1) * gw]).astype(BF16)


def ssd_branch(proj3, dt3, conv_w, conv_b, dt_bias_p, a_log_p, dskip_e, norm_w,
               xbc_col0, z_col0, d_inner, n_heads):
    b, s, _ = proj3.shape
    L = SSD_CHUNK
    xw = conv_w.shape[1]
    kern = functools.partial(_ssd_kernel, d_inner=d_inner, n_heads=n_heads)
    cst = lambda bi, ci: (0, 0)
    return pl.pallas_call(
        kern,
        grid=(b, s // L),
        in_specs=[pl.BlockSpec((1, L, xw), lambda bi, ci: (bi, ci, xbc_col0 // xw)),
                  pl.BlockSpec((1, L, d_inner), lambda bi, ci: (bi, ci, z_col0 // d_inner)),
                  pl.BlockSpec((1, L, 128), lambda bi, ci: (bi, ci, 0)),
                  pl.BlockSpec((SSD_CONV, xw), cst),
                  pl.BlockSpec((1, xw), cst),
                  pl.BlockSpec((1, 128), cst),
                  pl.BlockSpec((1, 128), cst),
                  pl.BlockSpec((1, d_inner), cst),
                  pl.BlockSpec((1, d_inner), cst)],
        out_specs=pl.BlockSpec((1, L, d_inner), lambda bi, ci: (bi, ci, 0)),
        out_shape=jax.ShapeDtypeStruct((b, s, d_inner), BF16),
        scratch_shapes=[pltpu.VMEM((L + 8, xw), F32),
                        pltpu.VMEM((SSD_GROUPS, SSD_D_STATE, d_inner // SSD_GROUPS), F32),
                        pltpu.VMEM((L, d_inner), F32)],
        compiler_params=_cparams(("parallel", "arbitrary")),
        name="ssd_scan",
    )(proj3, proj3, dt3, conv_w, conv_b, dt_bias_p, a_log_p, dskip_e, norm_w)


def _layer_norm(v, g, b):
    mu = jnp.mean(v, axis=-1, keepdims=True)
    d = v - mu
    var = jnp.mean(d * d, axis=-1, keepdims=True)
    return d * lax.rsqrt(var + EPS) * g + b


def _merge_kernel(att_ref, ssd_ref, g_ref, x_ref, gate_ref, sh2_ref, sc2_ref,
                  wa_ref, ws_ref, wo_ref, lg_ref, lb_ref, x1_ref, h2_ref, *, d):
    ya = _dot(att_ref[...], wa_ref[...])
    ys = _dot(ssd_ref[...], ws_ref[...])
    gl = jax.nn.sigmoid(g_ref[...].astype(F32))
    mixed = gl[:, :d] * ya + gl[:, d:] * ys
    mo = _dot(mixed.astype(BF16), wo_ref[...])
    x1 = _layer_norm(ALPHA * x_ref[...] + gate_ref[0] * mo, lg_ref[...], lb_ref[...])
    x1_ref[...] = x1
    h2_ref[...] = x1 * (1.0 + sc2_ref[0]) + sh2_ref[0]


def merge_ln1(att2d, ssd2d, proj2d, x2d, gate1, shift2, scale2, w_a, w_s, w_o, ln_g, ln_b,
              g_col0, row0):
    n, d = att2d.shape[0], x2d.shape[1]
    tm = 512
    blk0 = row0 // tm
    cst3 = lambda i: (0, 0, 0)
    cst = lambda i: (0, 0)
    kern = functools.partial(_merge_kernel, d=d)
    return pl.pallas_call(
        kern,
        grid=(n // tm,),
        in_specs=[pl.BlockSpec((tm, att2d.shape[1]), lambda i: (i, 0)),
                  pl.BlockSpec((tm, ssd2d.shape[1]), lambda i: (i + blk0, 0)),
                  pl.BlockSpec((tm, 2 * d), lambda i: (i + blk0, g_col0 // (2 * d))),
                  pl.BlockSpec((tm, d), lambda i: (i + blk0, 0)),
                  pl.BlockSpec((1, 1, d), cst3),
                  pl.BlockSpec((1, 1, d), cst3),
                  pl.BlockSpec((1, 1, d), cst3),
                  pl.BlockSpec(w_a.shape, cst),
                  pl.BlockSpec(w_s.shape, cst),
                  pl.BlockSpec(w_o.shape, cst),
                  pl.BlockSpec((1, d), cst),
                  pl.BlockSpec((1, d), cst)],
        out_specs=[pl.BlockSpec((tm, d), lambda i: (i, 0)),
                   pl.BlockSpec((tm, d), lambda i: (i, 0))],
        out_shape=[jax.ShapeDtypeStruct((n, d), F32),
                   jax.ShapeDtypeStruct((n, d), F32)],
        compiler_params=_cparams(("parallel",)),
        name="merge_ln1",
    )(att2d, ssd2d, proj2d, x2d, gate1, shift2, scale2, w_a, w_s, w_o, ln_g, ln_b)


ROUTE_T = 256


def _topk_rounds(scs, ids, k, payloads=None):
    n_p = len(scs)
    vals = [[] for _ in range(n_p)]
    sels = [[] for _ in range(n_p)]
    picks = [[] for _ in range(n_p)]
    scs = list(scs)
    for _ in range(k):
        ms = [jnp.max(sc, axis=0, keepdims=True) for sc in scs]
        ss = [jnp.min(jnp.where(scs[p] == ms[p], ids, float(1 << 24)), axis=0, keepdims=True)
              for p in range(n_p)]
        hits = [ids == ss[p] for p in range(n_p)]
        if payloads is not None:
            for p in range(n_p):
                picks[p].append(jnp.max(jnp.where(hits[p], payloads[p], -1.0), axis=0, keepdims=True))
        scs = [jnp.where(hits[p], -jnp.inf, scs[p]) for p in range(n_p)]
        for p in range(n_p):
            vals[p].append(ms[p])
            sels[p].append(ss[p])
    cat = lambda xs: jnp.concatenate(xs, axis=0)
    return [(cat(vals[p]), cat(sels[p]), cat(picks[p]) if payloads is not None else None)
            for p in range(n_p)]


STAIR_COUNT = (16, 8, 5, 4, 3, 2, 2, 2)
STAIR_ROWS = 16 + 8 * 7 + 8


def _stair_tables(t):
    K = PEER_TOPK
    r = lax.broadcasted_iota(jnp.int32, (STAIR_ROWS, t), 0)
    mid_a = 1 + ((r - K) >> 3)
    mid_b = (r - K) & 7
    a = jnp.where(r < K, 0, jnp.where(r < K + 56, mid_a, r - (K + 56) + 8))
    b = jnp.where(r < K, r, jnp.where(r < K + 56, mid_b, 0))
    count = jnp.full((STAIR_ROWS, t), 1, jnp.int32)
    for av, nb in enumerate(STAIR_COUNT):
        count = jnp.where(a == av, nb, count)
    valid = b < count
    pos_ids = (a * K + b).astype(F32)
    return valid, pos_ids


def _stair_candidates(v0, v1, i0, i1, valid):
    K = PEER_TOPK
    s_blocks = [v0[0:1, :] + v1]
    c_blocks = [i0[0:1, :] * float(PEER_N_KEYS) + i1]
    for a in range(1, 8):
        s_blocks.append(v0[a:a + 1, :] + v1[0:8, :])
        c_blocks.append(i0[a:a + 1, :] * float(PEER_N_KEYS) + i1[0:8, :])
    s_blocks.append(v0[8:K, :] + v1[0:1, :])
    c_blocks.append(i0[8:K, :] * float(PEER_N_KEYS) + i1[0:1, :])
    cand_s = jnp.where(valid, jnp.concatenate(s_blocks, axis=0), -jnp.inf)
    cand_i = jnp.concatenate(c_blocks, axis=0)
    return cand_s, cand_i


def _route_kernel(h2_ref, wq_ref, keys_ref, eidx_ref, gate_ref, q_ref, val_ref, idx_ref,
                  es_ref, gs_ref):
    t = ROUTE_T
    K = PEER_TOPK
    hb = h2_ref[...].astype(BF16)
    q = _dot(hb, wq_ref[...]).astype(BF16)
    for hj in range(2 * PEER_HEADS):
        q_ref[hj] = q[:, hj * PEER_HALF_DIM:(hj + 1) * PEER_HALF_DIM]

    key_ids = lax.broadcasted_iota(jnp.int32, (PEER_N_KEYS, t), 0).astype(F32)

    def level1(h, carry):
        scs = [_dot_nt(keys_ref[j], q_ref[2 * h + j]) for j in range(2)]
        res = _topk_rounds(scs, key_ids, K)
        for j in range(2):
            val_ref[2 * h + j] = res[j][0]
            idx_ref[2 * h + j] = res[j][1]
        return carry

    lax.fori_loop(0, PEER_HEADS, level1, 0)

    valid, pos_ids = _stair_tables(t)

    def level2(hp, carry):
        heads = [2 * hp, 2 * hp + 1]
        cands = [_stair_candidates(val_ref[2 * h], val_ref[2 * h + 1], idx_ref[2 * h], idx_ref[2 * h + 1], valid)
                 for h in heads]
        res = _topk_rounds([c[0] for c in cands], pos_ids, K, payloads=[c[1] for c in cands])
        for p, h in enumerate(heads):
            top_s = res[p][0]
            e = jnp.exp(top_s - top_s[0:1, :])
            gs_ref[h] = e / jnp.sum(e, axis=0, keepdims=True)
            es_ref[h] = res[p][2]
        return carry

    lax.fori_loop(0, PEER_HEADS // 2, level2, 0)

    eidx_ref[...] = es_ref[...].reshape(PEER_HEADS * K, t).T.astype(jnp.int32)
    gate_ref[...] = gs_ref[...].reshape(PEER_HEADS * K, t).T


def peer_route(h2, w_query, sub_keys):
    n, d = h2.shape
    t = ROUTE_T
    K = PEER_TOPK
    nsel = PEER_HEADS * K
    cst2 = lambda i: (0, 0)
    return pl.pallas_call(
        _route_kernel,
        grid=(n // t,),
        in_specs=[pl.BlockSpec((t, d), lambda i: (i, 0)),
                  pl.BlockSpec(w_query.shape, cst2),
                  pl.BlockSpec(sub_keys.shape, lambda i: (0, 0, 0))],
        out_specs=[pl.BlockSpec((t, nsel), lambda i: (i, 0)),
                   pl.BlockSpec((t, nsel), lambda i: (i, 0))],
        out_shape=[jax.ShapeDtypeStruct((n, nsel), jnp.int32),
                   jax.ShapeDtypeStruct((n, nsel), F32)],
        scratch_shapes=[pltpu.VMEM((2 * PEER_HEADS, t, PEER_HALF_DIM), BF16),
                        pltpu.VMEM((2 * PEER_HEADS, K, t), F32),
                        pltpu.VMEM((2 * PEER_HEADS, K, t), F32),
                        pltpu.VMEM((PEER_HEADS, K, t), F32),
                        pltpu.VMEM((PEER_HEADS, K, t), F32)],
        compiler_params=_cparams(("parallel",)),
        name="peer_route",
    )(h2, w_query, sub_keys)


SC_CORES = 2
SC_SUBCORES = 16
SC_WORKERS = SC_CORES * SC_SUBCORES
SC_LANES = 16
GATHER_ROWS = 32
TOKEN_BLOCK = 8
COMBINE_COLS = 2
DOT_ROWS = 16
PARTS_PER_ROW = 128 // SC_LANES


def pack_table(tab):
    e, d = tab.shape
    bits = lax.bitcast_convert_type(tab.astype(BF16), jnp.uint16).astype(jnp.uint32)
    bits = bits.reshape(e, d // (2 * SC_LANES), 2, SC_LANES)
    packed = bits[:, :, 0, :] | (bits[:, :, 1, :] << 16)
    return lax.bitcast_convert_type(packed, jnp.int32).reshape(e, d // 2)


def _unpack_pair(w):
    lo = lax.bitcast_convert_type(jnp.left_shift(w, 16), F32)
    hi = lax.bitcast_convert_type(jnp.bitwise_and(w, jnp.int32(-65536)), F32)
    return lo, hi


def _sc_mesh():
    return plsc.VectorSubcoreMesh(core_axis_name="c", subcore_axis_name="s")


def _worker_id():
    return lax.axis_index("s") * SC_CORES + lax.axis_index("c")


def _gather_pipeline(tab_hbm, idx_v, rows_v, sems, n_chunks, chunks_per_tok, compute):
    def gather(q, buf):
        tok = q // chunks_per_tok
        ch = q % chunks_per_tok
        return pltpu.make_async_copy(
            tab_hbm.at[idx_v.at[tok, pl.ds(ch * GATHER_ROWS, GATHER_ROWS)]],
            rows_v.at[buf], sems.at[buf])

    gather(0, 0).start()

    def pair(i, carry):
        q0 = 2 * i
        gather(q0 + 1, 1).start()
        gather(q0, 0).wait()
        compute(q0, 0)

        @pl.when(q0 + 2 < n_chunks)
        def _():
            gather(q0 + 2, 0).start()

        gather(q0 + 1, 1).wait()
        compute(q0 + 1, 1)
        return carry

    lax.fori_loop(0, n_chunks // 2, pair, 0)


def _dots_phase(u_hbm, h2_hbm, eidx_hbm, out_hbm, idx_v, x_v, rows_v, o_v, sems, n, d, nsel):
    tok_per_w = n // SC_WORKERS
    chunks_per_tok = nsel // GATHER_ROWS
    n_chunks = TOKEN_BLOCK * chunks_per_tok
    rows_per_chunk = GATHER_ROWS // PARTS_PER_ROW
    out_rows = TOKEN_BLOCK * nsel // PARTS_PER_ROW
    wid = _worker_id()

    def compute(q, buf):
        tok = q // chunks_per_tok
        zero = jnp.zeros((SC_LANES,), F32)
        for r0 in range(0, GATHER_ROWS, DOT_ROWS):
            def cbody(c, accs, r0=r0):
                x_lo = x_v[tok, pl.ds(c * 2 * SC_LANES, SC_LANES)]
                x_hi = x_v[tok, pl.ds(c * 2 * SC_LANES + SC_LANES, SC_LANES)]
                out = []
                for r in range(DOT_ROWS):
                    lo, hi = _unpack_pair(rows_v[buf, r0 + r, pl.ds(c * SC_LANES, SC_LANES)])
                    out.append(accs[r] + lo * x_lo + hi * x_hi)
                return tuple(out)

            accs = lax.fori_loop(0, d // (2 * SC_LANES), cbody, tuple(zero for _ in range(DOT_ROWS)))
            for r in range(DOT_ROWS):
                rr = r0 + r
                o_v[q * rows_per_chunk + rr // PARTS_PER_ROW,
                    pl.ds((rr % PARTS_PER_ROW) * SC_LANES, SC_LANES)] = accs[r]

    def block(bi, carry):
        t0 = wid * tok_per_w + bi * TOKEN_BLOCK
        pltpu.sync_copy(eidx_hbm.at[pl.ds(t0, TOKEN_BLOCK)], idx_v)
        pltpu.sync_copy(h2_hbm.at[pl.ds(t0, TOKEN_BLOCK)], x_v)
        _gather_pipeline(u_hbm, idx_v, rows_v, sems, n_chunks, chunks_per_tok, compute)
        pltpu.sync_copy(o_v, out_hbm.at[pl.ds(t0 * (nsel // PARTS_PER_ROW), out_rows)])
        return carry

    lax.fori_loop(0, tok_per_w // TOKEN_BLOCK, block, 0)


def _combine_phase(v_hbm, w_hbm, eidx_hbm, out_hbm, idx_v, w_v, rows_v, y_v, sems, n, d, nsel):
    tok_per_w = n // SC_WORKERS
    chunks_per_tok = nsel // GATHER_ROWS
    n_chunks = TOKEN_BLOCK * chunks_per_tok
    wid = _worker_id()

    def compute(q, buf):
        tok = q // chunks_per_tok
        ch = q % chunks_per_tok
        tok_vec = jnp.full((SC_LANES,), tok, jnp.int32)
        ws = [plsc.load_gather(
            w_v, [tok_vec, jnp.full((SC_LANES,), ch * GATHER_ROWS + r, jnp.int32)])
            for r in range(GATHER_ROWS)]
        first = ch == 0

        def cbody(cg, carry):
            cols = [cg * COMBINE_COLS + u for u in range(COMBINE_COLS)]
            sl_lo = [pl.ds(c * 2 * SC_LANES, SC_LANES) for c in cols]
            sl_hi = [pl.ds(c * 2 * SC_LANES + SC_LANES, SC_LANES) for c in cols]
            zero = jnp.zeros((SC_LANES,), F32)
            a_lo = [jnp.where(first, zero, y_v[tok, s]) for s in sl_lo]
            a_hi = [jnp.where(first, zero, y_v[tok, s]) for s in sl_hi]
            for r in range(GATHER_ROWS):
                for u in range(COMBINE_COLS):
                    lo, hi = _unpack_pair(rows_v[buf, r, pl.ds(cols[u] * SC_LANES, SC_LANES)])
                    a_lo[u] = a_lo[u] + ws[r] * lo
                    a_hi[u] = a_hi[u] + ws[r] * hi
            for u in range(COMBINE_COLS):
                y_v[tok, sl_lo[u]] = a_lo[u]
                y_v[tok, sl_hi[u]] = a_hi[u]
            return carry

        lax.fori_loop(0, d // (2 * SC_LANES) // COMBINE_COLS, cbody, 0)

    def block(bi, carry):
        t0 = wid * tok_per_w + bi * TOKEN_BLOCK
        pltpu.sync_copy(eidx_hbm.at[pl.ds(t0, TOKEN_BLOCK)], idx_v)
        pltpu.sync_copy(w_hbm.at[pl.ds(t0, TOKEN_BLOCK)], w_v)
        _gather_pipeline(v_hbm, idx_v, rows_v, sems, n_chunks, chunks_per_tok, compute)
        pltpu.sync_copy(y_v, out_hbm.at[pl.ds(t0, TOKEN_BLOCK)])
        return carry

    lax.fori_loop(0, tok_per_w // TOKEN_BLOCK, block, 0)


def peer_experts_sc(u_pk=None, h2=None, eidx=None, v_pk=None, wts_prev=None, eidx_prev=None):
    do_dots = u_pk is not None
    do_comb = v_pk is not None
    ref = h2 if do_dots else wts_prev
    n = ref.shape[0]
    nsel = (eidx if do_dots else eidx_prev).shape[1]
    d = 2 * (u_pk if do_dots else v_pk).shape[1]
    out_types, scratch, args = [], [], []
    if do_comb:
        out_types.append(jax.ShapeDtypeStruct((n, d), F32))
        args += [v_pk, wts_prev, eidx_prev]
        scratch += [pltpu.VMEM((TOKEN_BLOCK, nsel), F32), pltpu.VMEM((TOKEN_BLOCK, d), F32)]
    if do_dots:
        out_types.append(jax.ShapeDtypeStruct((n * nsel // PARTS_PER_ROW, 128), F32))
        args += [u_pk, h2, eidx]
        scratch += [pltpu.VMEM((TOKEN_BLOCK, d), F32),
                    pltpu.VMEM((TOKEN_BLOCK * nsel // PARTS_PER_ROW, 128), F32)]
    scratch += [pltpu.VMEM((TOKEN_BLOCK, nsel), jnp.int32),
                pltpu.VMEM((2, GATHER_ROWS, d // 2), jnp.int32),
                pltpu.SemaphoreType.DMA((2,))]

    def body(*refs):
        refs = list(refs)
        ins = refs[:len(args)]
        outs = refs[len(args):len(args) + len(out_types)]
        scr = refs[len(args) + len(out_types):]
        idx_v, rows_v, sems = scr[-3:]
        if do_comb:
            v_hbm, w_hbm, ep_hbm = ins[:3]
            _combine_phase(v_hbm, w_hbm, ep_hbm, outs[0], idx_v, scr[0], rows_v, scr[1], sems, n, d, nsel)
        if do_dots:
            u_hbm, h2_hbm, e_hbm = ins[-3:]
            x_v, o_v = scr[-5], scr[-4]
            _dots_phase(u_hbm, h2_hbm, e_hbm, outs[-1], idx_v, x_v, rows_v, o_v, sems, n, d, nsel)

    res = pl.kernel(
        body, out_type=tuple(out_types), mesh=_sc_mesh(), scratch_types=scratch,
        compiler_params=pltpu.CompilerParams(needs_layout_passes=False),
        name="peer_experts_sc" + ("_c" if do_comb else "") + ("_d" if do_dots else ""),
    )(*args)
    res = list(res)
    y_prev = res[0] if do_comb else None
    part = res[-1] if do_dots else None
    return y_prev, part


def _weights_kernel(part_ref, gate_ref, red_ref, o_ref):
    act = jnp.dot(part_ref[...], red_ref[...], preferred_element_type=F32,
                  precision=lax.Precision.HIGHEST)
    gelu = 0.5 * act * (1.0 + lax.erf(act * (2.0 ** -0.5)))
    o_ref[...] = gate_ref[...] * gelu


def peer_weights(part2d, gate):
    n, nsel = gate.shape
    tm = 512
    w = part2d.shape[1]
    red = (jnp.arange(w)[:, None] // SC_LANES == jnp.arange(nsel)[None, :]).astype(F32)
    return pl.pallas_call(
        _weights_kernel,
        grid=(n // tm,),
        in_specs=[pl.BlockSpec((tm, w), lambda i: (i, 0)),
                  pl.BlockSpec((tm, nsel), lambda i: (i, 0)),
                  pl.BlockSpec((w, nsel), lambda i: (0, 0))],
        out_specs=pl.BlockSpec((tm, nsel), lambda i: (i, 0)),
        out_shape=jax.ShapeDtypeStruct((n, nsel), F32),
        compiler_params=_cparams(("parallel",)),
        name="peer_weights",
    )(part2d, gate, red)


def _final_kernel(x1_ref, y_ref, gate_ref, lg_ref, lb_ref, o_ref):
    o_ref[...] = _layer_norm(ALPHA * x1_ref[...] + gate_ref[0] * y_ref[...], lg_ref[...], lb_ref[...])


def final_ln(x1, y_ffn, gate2, ln_g, ln_b, seq):
    n, d = x1.shape
    tm = 1024
    tiles_per_batch = seq // tm
    return pl.pallas_call(
        _final_kernel,
        grid=(n // tm,),
        in_specs=[pl.BlockSpec((tm, d), lambda i: (i, 0)),
                  pl.BlockSpec((tm, d), lambda i: (i, 0)),
                  pl.BlockSpec((1, 1, d), lambda i: (i // tiles_per_batch, 0, 0)),
                  pl.BlockSpec((1, d), lambda i: (0, 0)),
                  pl.BlockSpec((1, d), lambda i: (0, 0))],
        out_specs=pl.BlockSpec((tm, d), lambda i: (i, 0)),
        out_shape=jax.ShapeDtypeStruct((n, d), F32),
        compiler_params=_cparams(("parallel",)),
        name="final_ln",
    )(x1, y_ffn, gate2, ln_g, ln_b)


def _cumsum(widths):
    out, t = [], 0
    for w in widths:
        t += w
        out.append(t)
    return out


def _block(x, c, w_ada, b_ada, w_in, conv_w, conv_b, dt_bias, a_log, d_skip, ssd_norm_w,
           lambda_q1, lambda_k1, lambda_q2, lambda_k2, da_subln_w, w_attn_branch, w_ssd_branch,
           w_out, ln1_g, ln1_b, peer_w_query, peer_sub_keys, peer_u, peer_v, ln2_g, ln2_b,
           layer):
    bsz, s, d = x.shape
    qk_w = DA_HEADS * 2 * DA_HEAD_DIM
    v_w = DA_HEADS * DA_V_DIM
    d_inner = w_ssd_branch.shape[0]
    n_heads = d_inner // SSD_HEAD_DIM
    bc_w = SSD_GROUPS * SSD_D_STATE
    xbc_w = d_inner + 2 * bc_w

    c_pad = jnp.pad(c, ((0, 8 - bsz), (0, 0)))
    mod = ada_mod(c_pad, w_ada, b_ada)[:bsz]
    shift1, scale1, gate1, shift2, scale2, gate2 = [m.reshape(bsz, 1, d) for m in jnp.split(mod, 6, axis=-1)]

    splits = _cumsum([qk_w, qk_w, v_w, d_inner, xbc_w, n_heads, 2 * d])
    wq, wk, wv, wz, wxbc, wdt, wg = [w_in[:, a:b] for a, b in zip([0] + splits[:-1], splits)]
    w_main = jnp.concatenate([wxbc, wq, wz, wg, wk, wv], axis=1).astype(BF16)
    w_dt = jnp.pad(wdt, ((0, 0), (0, 128 - n_heads))).astype(BF16)
    xbc_c0 = 0
    q_c0 = xbc_w
    z_c0 = q_c0 + qk_w
    g_c0 = z_c0 + d_inner
    k_c0 = g_c0 + 2 * d
    v_c0 = k_c0 + qk_w
    lambda_init = 0.8 - 0.6 * math.exp(-0.3 * layer)
    lam_params = jnp.stack([lambda_q1, lambda_k1, lambda_q2, lambda_k2]).astype(F32)
    pad_h = lambda t: jnp.pad(t.astype(F32), (0, 128 - n_heads)).reshape(1, 128)
    dtb_p, alog_p = pad_h(dt_bias), pad_h(a_log)
    dskip_e = jnp.repeat(d_skip.astype(F32), SSD_HEAD_DIM).reshape(1, d_inner)
    w_ab, w_sb, w_o = w_attn_branch.astype(BF16), w_ssd_branch.astype(BF16), w_out.astype(BF16)
    w_qp, keys_b = peer_w_query.astype(BF16), peer_sub_keys.astype(BF16)
    u_pk, v_pk = pack_table(peer_u), pack_table(peer_v)

    gs = s // SEQ_SPLIT
    outs = []
    pending = None

    def retire(cur):
        nonlocal pending
        kw = {}
        if pending is not None:
            p_x1, p_gate2, p_eidx, p_gate, p_part = pending
            wts = peer_weights(p_part.reshape(gs, -1), p_gate)
            kw.update(v_pk=v_pk, wts_prev=wts, eidx_prev=p_eidx)
        if cur is not None:
            kw.update(u_pk=u_pk, h2=cur[4], eidx=cur[2])
        y_prev, part = peer_experts_sc(**kw)
        if pending is not None:
            outs.append(final_ln(p_x1, y_prev, p_gate2, ln2_g.reshape(1, d), ln2_b.reshape(1, d), gs))
        pending = None if cur is None else (cur[0], cur[1], cur[2], cur[3], part)

    for b in range(bsz):
        sl = slice(b, b + 1)
        xb = x[b]
        proj, dt_raw = in_proj(xb, shift1[sl], scale1[sl], w_main, w_dt, s)
        proj3 = proj.reshape(1, s, proj.shape[1])
        y_ssd = ssd_branch(proj3, dt_raw.reshape(1, s, 128), conv_w, conv_b.reshape(1, xbc_w),
                           dtb_p, alog_p, dskip_e, ssd_norm_w.reshape(1, d_inner),
                           xbc_c0, z_c0, d_inner, n_heads).reshape(s, d_inner)
        for g in range(SEQ_SPLIT):
            y_attn = diff_attention(proj3, lam_params, da_subln_w.reshape(1, DA_V_DIM), q_c0, k_c0, v_c0,
                                    lambda_init, g * (gs // ATT_TQ), gs // ATT_TQ)
            x1, h2 = merge_ln1(y_attn.reshape(gs, v_w), y_ssd, proj, xb,
                               gate1[sl], shift2[sl], scale2[sl], w_ab, w_sb, w_o,
                               ln1_g.reshape(1, d), ln1_b.reshape(1, d), g_c0, g * gs)
            eidx, gate = peer_route(h2, w_qp, keys_b)
            retire((x1, gate2[sl], eidx, gate, h2))
    retire(None)
    return jnp.concatenate(outs, axis=0).reshape(bsz, s, d)


def kernel(x, c, w_ada, b_ada, w_in, conv_w, conv_b, dt_bias, a_log, d_skip, ssd_norm_w,
           lambda_q1, lambda_k1, lambda_q2, lambda_k2, da_subln_w, w_attn_branch, w_ssd_branch,
           w_out, ln1_g, ln1_b, peer_w_query, peer_sub_keys, peer_u, peer_v, ln2_g, ln2_b):
    for l in range(w_ada.shape[0]):
        x = _block(x, c, w_ada[l], b_ada[l], w_in[l], conv_w[l], conv_b[l], dt_bias[l], a_log[l],
                   d_skip[l], ssd_norm_w[l], lambda_q1[l], lambda_k1[l], lambda_q2[l], lambda_k2[l],
                   da_subln_w[l], w_attn_branch[l], w_ssd_branch[l], w_out[l], ln1_g[l], ln1_b[l],
                   peer_w_query[l], peer_sub_keys[l], peer_u[l], peer_v[l], ln2_g[l], ln2_b[l], l)
    return x
```

```python
import functools
import math

import jax
import jax.numpy as jnp
from jax import lax
from jax.experimental import pallas as pl
from jax.experimental.pallas import tpu as pltpu
from jax.experimental.pallas import tpu_sc as plsc

F32 = jnp.float32
BF16 = jnp.bfloat16

DA_HEADS = 8
DA_HEAD_DIM = 64
DA_V_DIM = 2 * DA_HEAD_DIM
SSD_HEAD_DIM = 64
SSD_GROUPS = 4
SSD_D_STATE = 128
SSD_CONV = 4
SSD_CHUNK = 256
PEER_N_KEYS = 128
PEER_HEADS = 8
PEER_TOPK = 16
PEER_HALF_DIM = 128
DEPTH = 1
ALPHA = (2 * DEPTH) ** 0.25
EPS = 1e-5
LOG2E = 1.4426950408889634
NEG_BIG = -1e30

VMEM_LIMIT_BYTES = 56 * 1024 * 1024


def _cparams(sem):
    return pltpu.CompilerParams(dimension_semantics=sem, vmem_limit_bytes=VMEM_LIMIT_BYTES)


def _dot(a, b):
    return jnp.dot(a, b, preferred_element_type=F32)


def _dot_nt(a, b):
    return lax.dot_general(a, b, (((1,), (1,)), ((), ())), preferred_element_type=F32)


def _ada_kernel(c_ref, w_ref, b_ref, o_ref):
    c = c_ref[...]
    sc = c * jax.nn.sigmoid(c)
    o_ref[...] = jnp.dot(sc, w_ref[...], preferred_element_type=F32,
                         precision=lax.Precision.HIGHEST) + b_ref[...]


def ada_mod(c_pad, w_ada, b_ada):
    d, n = w_ada.shape
    tn = 512
    return pl.pallas_call(
        _ada_kernel,
        grid=(n // tn,),
        in_specs=[pl.BlockSpec((c_pad.shape[0], d), lambda j: (0, 0)),
                  pl.BlockSpec((d, tn), lambda j: (0, j)),
                  pl.BlockSpec((1, tn), lambda j: (0, j))],
        out_specs=pl.BlockSpec((c_pad.shape[0], tn), lambda j: (0, j)),
        out_shape=jax.ShapeDtypeStruct((c_pad.shape[0], n), F32),
        compiler_params=_cparams(("arbitrary",)),
        name="ada_mod",
    )(c_pad, w_ada, b_ada.reshape(1, n))


def _inproj_kernel(x_ref, sh_ref, sc_ref, w_ref, wdt_ref, o_ref, dt_ref, h_ref):
    @pl.when(pl.program_id(1) == 0)
    def _():
        h = x_ref[...] * (1.0 + sc_ref[0]) + sh_ref[0]
        hb = h.astype(BF16)
        h_ref[...] = hb
        dt_ref[...] = _dot(hb, wdt_ref[...])

    o_ref[...] = _dot(h_ref[...], w_ref[...]).astype(BF16)


def in_proj(x2d, shift1, scale1, w_main, w_dt, seq):
    n, d = x2d.shape
    tm, tn = 1024, 1024
    tiles_per_batch = seq // tm
    width = w_main.shape[1]
    bmap = lambda i, j: (i // tiles_per_batch, 0, 0)
    return pl.pallas_call(
        _inproj_kernel,
        grid=(n // tm, width // tn),
        in_specs=[pl.BlockSpec((tm, d), lambda i, j: (i, 0)),
                  pl.BlockSpec((1, 1, d), bmap),
                  pl.BlockSpec((1, 1, d), bmap),
                  pl.BlockSpec((d, tn), lambda i, j: (0, j)),
                  pl.BlockSpec((d, 128), lambda i, j: (0, 0))],
        out_specs=[pl.BlockSpec((tm, tn), lambda i, j: (i, j)),
                   pl.BlockSpec((tm, 128), lambda i, j: (i, 0))],
        out_shape=[jax.ShapeDtypeStruct((n, width), BF16),
                   jax.ShapeDtypeStruct((n, 128), F32)],
        scratch_shapes=[pltpu.VMEM((tm, d), BF16)],
        compiler_params=_cparams(("parallel", "arbitrary")),
        name="in_proj",
    )(x2d, shift1, scale1, w_main, w_dt)


ATT_TQ = 256
ATT_TK = 512


def _attn_kernel(q_ref, k_ref, v_ref, lam_ref, w_ref, o_ref,
                 vt_ref, sa_ref, sb_ref, bias_ref, m_ref, l_ref, acc_ref, *, kv_len, q_block0, lambda_init):
    tq, tk = ATT_TQ, ATT_TK
    h = pl.program_id(1)
    qi = pl.program_id(2) + q_block0
    n_kb_total = kv_len // tk

    @pl.when(pl.program_id(2) == 0)
    def _():
        for c in range(n_kb_total):
            blk = v_ref[0, c * tk:(c + 1) * tk, :].astype(F32)
            vt_ref[c] = blk.T.astype(BF16)

    slope2 = jnp.exp2(-(h + 1).astype(F32)) * LOG2E
    i0 = qi * tq
    kb_diag = i0 // tk
    q = q_ref[0]
    qs = [q[:, c * DA_HEAD_DIM:(c + 1) * DA_HEAD_DIM] for c in range(2)]
    rows = lax.broadcasted_iota(jnp.int32, (tk, tq), 0)
    cols = lax.broadcasted_iota(jnp.int32, (tk, tq), 1)
    bias0 = rows.astype(F32) * slope2
    bias_ref[0] = bias0
    bias_ref[1] = jnp.where(rows + kb_diag * tk <= cols + i0, bias0, NEG_BIG)

    m_ref[...] = jnp.full(m_ref.shape, NEG_BIG, F32)
    l_ref[...] = jnp.zeros(l_ref.shape, F32)
    acc_ref[...] = jnp.zeros(acc_ref.shape, F32)

    def scores(kb, s_ref):
        kblk = k_ref[0, pl.ds(pl.multiple_of(kb * tk, tk), tk), :]
        for c in range(2):
            s_ref[c] = _dot_nt(kblk[:, c * DA_HEAD_DIM:(c + 1) * DA_HEAD_DIM], qs[c])

    def softmax_pv(kb, s_ref):
        vt = vt_ref[kb]
        cb = slope2 * (kb * tk - i0).astype(F32)
        bias = bias_ref[(kb == kb_diag).astype(jnp.int32)]
        for c in range(2):
            s = s_ref[c] + bias
            m_old = m_ref[c]
            m_new = jnp.maximum(m_old, jnp.max(s, axis=0, keepdims=True) + cb)
            p = jnp.exp2(s - (m_new - cb))
            alpha = jnp.exp2(m_old - m_new)
            l_ref[c] = alpha * l_ref[c] + jnp.sum(p, axis=0, keepdims=True)
            acc_ref[c] = alpha * acc_ref[c] + _dot(vt, p.astype(BF16))
            m_ref[c] = m_new

    n_blocks = kb_diag + 1
    scores(0, sa_ref)

    def body(i, carry):
        kb = 2 * i
        scores(kb + 1, sb_ref)
        softmax_pv(kb, sa_ref)
        scores(jnp.minimum(kb + 2, kb_diag), sa_ref)
        softmax_pv(kb + 1, sb_ref)
        return carry

    lax.fori_loop(0, n_blocks // 2, body, 0)

    @pl.when(n_blocks % 2 == 1)
    def _():
        softmax_pv(kb_diag, sa_ref)

    lam_p = lam_ref[...]
    lam = (jnp.exp(jnp.sum(lam_p[0:1] * lam_p[1:2])) - jnp.exp(jnp.sum(lam_p[2:3] * lam_p[3:4]))
           + lambda_init)
    o_t = acc_ref[0] / l_ref[0] - lam * (acc_ref[1] / l_ref[1])
    o = o_t.T
    o = o * lax.rsqrt(jnp.mean(o * o, axis=-1, keepdims=True) + EPS) * w_ref[...]
    o_ref[0] = (o * (1.0 - lambda_init)).astype(BF16)


def diff_attention(proj3, lam_params, subln_w, q_col0, k_col0, v_col0, lambda_init, q_block0, n_q_blocks):
    b, s, _ = proj3.shape
    tq, tk = ATT_TQ, ATT_TK
    kv_len = -(-((q_block0 + n_q_blocks) * tq) // tk) * tk
    qb, kb, vb = q_col0 // 128, k_col0 // 128, v_col0 // 128
    kern = functools.partial(_attn_kernel, kv_len=kv_len, q_block0=q_block0, lambda_init=lambda_init)
    return pl.pallas_call(
        kern,
        grid=(b, DA_HEADS, n_q_blocks),
        in_specs=[pl.BlockSpec((1, tq, 128), lambda bi, h, qi: (bi, qi + q_block0, qb + h)),
                  pl.BlockSpec((1, kv_len, 128), lambda bi, h, qi: (bi, 0, kb + h)),
                  pl.BlockSpec((1, kv_len, 128), lambda bi, h, qi: (bi, 0, vb + h)),
                  pl.BlockSpec((4, DA_HEAD_DIM), lambda bi, h, qi: (0, 0)),
                  pl.BlockSpec((1, DA_V_DIM), lambda bi, h, qi: (0, 0))],
        out_specs=pl.BlockSpec((1, tq, 128), lambda bi, h, qi: (bi, qi, h)),
        out_shape=jax.ShapeDtypeStruct((b, n_q_blocks * tq, DA_HEADS * DA_V_DIM), BF16),
        scratch_shapes=[pltpu.VMEM((kv_len // tk, DA_V_DIM, tk), BF16),
                        pltpu.VMEM((2, tk, tq), F32),
                        pltpu.VMEM((2, tk, tq), F32),
                        pltpu.VMEM((2, tk, tq), F32),
                        pltpu.VMEM((2, 1, tq), F32),
                        pltpu.VMEM((2, 1, tq), F32),
                        pltpu.VMEM((2, DA_V_DIM, tq), F32)],
        compiler_params=_cparams(("parallel", "arbitrary", "arbitrary")),
        name="diff_attn",
    )(proj3, proj3, proj3, lam_params, subln_w)


def _split3(x):
    hi = x.astype(BF16)
    r1 = x - hi.astype(F32)
    mid = r1.astype(BF16)
    lo = (r1 - mid.astype(F32)).astype(BF16)
    return hi, mid, lo


def _ssd_kernel(xbc_ref, z_ref, dt_ref, cw_ref, cb_ref, dtb_ref, alog_ref, dsk_ref, nw_ref,
                o_ref, ext_ref, state_ref, y_ref, *, d_inner, n_heads):
    L = SSD_CHUNK
    P = SSD_HEAD_DIM
    NS = SSD_D_STATE
    G = SSD_GROUPS
    R = n_heads // G
    bc_w = G * NS
    ci = pl.program_id(1)

    @pl.when(ci == 0)
    def _():
        state_ref[...] = jnp.zeros(state_ref.shape, F32)
        ext_ref[0:8, :] = jnp.zeros((8, ext_ref.shape[1]), F32)

    ext_ref[8:8 + L, :] = xbc_ref[0].astype(F32)
    conv = cb_ref[...]
    for k in range(SSD_CONV):
        off = 8 - (SSD_CONV - 1) + k
        conv = conv + cw_ref[k:k + 1, :] * ext_ref[off:off + L, :]
    ext_ref[0:8, :] = ext_ref[L:L + 8, :]
    xc = conv * jax.nn.sigmoid(conv)

    dtr = dt_ref[0] + dtb_ref[...]
    dt = jnp.maximum(dtr, 0.0) + jnp.log1p(jnp.exp(-jnp.abs(dtr)))
    a = -jnp.exp(alog_ref[...])
    da = dt * a
    tri = (lax.broadcasted_iota(jnp.int32, (L, L), 0) >= lax.broadcasted_iota(jnp.int32, (L, L), 1))
    tri_b = tri.astype(BF16)
    hi, mid, lo = _split3(da)
    cum = _dot(tri_b, hi) + _dot(tri_b, mid) + _dot(tri_b, lo)
    cum_t = cum.T
    dt_t = dt.T
    cum_last = cum[L - 1:L, :]
    e_cum = jnp.exp(cum)
    dec_end = jnp.exp(cum_last - cum) * dt
    e_last = jnp.exp(cum_last)

    for g in range(G):
        bg = xc[:, d_inner + g * NS:d_inner + (g + 1) * NS]
        cg = xc[:, d_inner + bc_w + g * NS:d_inner + bc_w + (g + 1) * NS]
        bg_b = bg.astype(BF16)
        cg_b = cg.astype(BF16)
        bgt_b = bg.T.astype(BF16)
        cbm = _dot_nt(cg_b, bg_b)
        st_g = state_ref[g]
        y_state = _dot(cg_b, st_g.astype(BF16))
        for r in range(R):
            hh = g * R + r
            xh = xc[:, hh * P:(hh + 1) * P]
            seg = cum[:, hh:hh + 1] - cum_t[hh:hh + 1, :]
            w = jnp.where(tri, jnp.exp(seg), 0.0) * cbm * dt_t[hh:hh + 1, :]
            yh = _dot(w.astype(BF16), xh.astype(BF16))
            yh = yh + y_state[:, r * P:(r + 1) * P] * e_cum[:, hh:hh + 1]
            y_ref[:, hh * P:(hh + 1) * P] = yh
            dx = (xh * dec_end[:, hh:hh + 1]).astype(BF16)
            state_ref[g, :, r * P:(r + 1) * P] = (
                st_g[:, r * P:(r + 1) * P] * e_last[:, hh:hh + 1] + _dot(bgt_b, dx))

    xs = xc[:, :d_inner]
    z = z_ref[0].astype(F32)
    y = (y_ref[...] + dsk_ref[...] * xs) * (z * jax.nn.sigmoid(z))
    gw = d_inner // G
    for g in range(G):
        yg = y[:, g * gw:(g + 1) * gw]
        yn = yg * lax.rsqrt(jnp.mean(yg * yg, axis=-1, keepdims=True) + EPS)
        o_ref[0, :, g * gw:(g + 1) * gw] = (yn * nw_ref[:, g * gw:(g + 1) * gw]).astype(BF16)


def ssd_branch(proj3, dt3, conv_w, conv_b, dt_bias_p, a_log_p, dskip_e, norm_w,
               xbc_col0, z_col0, d_inner, n_heads):
    b, s, _ = proj3.shape
    L = SSD_CHUNK
    xw = conv_w.shape[1]
    kern = functools.partial(_ssd_kernel, d_inner=d_inner, n_heads=n_heads)
    cst = lambda bi, ci: (0, 0)
    return pl.pallas_call(
        kern,
        grid=(b, s // L),
        in_specs=[pl.BlockSpec((1, L, xw), lambda bi, ci: (bi, ci, xbc_col0 // xw)),
                  pl.BlockSpec((1, L, d_inner), lambda bi, ci: (bi, ci, z_col0 // d_inner)),
                  pl.BlockSpec((1, L, 128), lambda bi, ci: (bi, ci, 0)),
                  pl.BlockSpec((SSD_CONV, xw), cst),
                  pl.BlockSpec((1, xw), cst),
                  pl.BlockSpec((1, 128), cst),
                  pl.BlockSpec((1, 128), cst),
                  pl.BlockSpec((1, d_inner), cst),
                  pl.BlockSpec((1, d_inner), cst)],
        out_specs=pl.BlockSpec((1, L, d_inner), lambda bi, ci: (bi, ci, 0)),
        out_shape=jax.ShapeDtypeStruct((b, s, d_inner), BF16),
        scratch_shapes=[pltpu.VMEM((L + 8, xw), F32),
                        pltpu.VMEM((SSD_GROUPS, SSD_D_STATE, d_inner // SSD_GROUPS), F32),
                        pltpu.VMEM((L, d_inner), F32)],
        compiler_params=_cparams(("parallel", "arbitrary")),
        name="ssd_scan",
    )(proj3, proj3, dt3, conv_w, conv_b, dt_bias_p, a_log_p, dskip_e, norm_w)


def _layer_norm(v, g, b):
    mu = jnp.mean(v, axis=-1, keepdims=True)
    d = v - mu
    var = jnp.mean(d * d, axis=-1, keepdims=True)
    return d * lax.rsqrt(var + EPS) * g + b


def _merge_kernel(att_ref, ssd_ref, g_ref, x_ref, gate_ref, sh2_ref, sc2_ref,
                  wa_ref, ws_ref, wo_ref, lg_ref, lb_ref, x1_ref, h2_ref, *, d):
    ya = _dot(att_ref[...], wa_ref[...])
    ys = _dot(ssd_ref[...], ws_ref[...])
    gl = jax.nn.sigmoid(g_ref[...].astype(F32))
    mixed = gl[:, :d] * ya + gl[:, d:] * ys
    mo = _dot(mixed.astype(BF16), wo_ref[...])
    x1 = _layer_norm(ALPHA * x_ref[...] + gate_ref[0] * mo, lg_ref[...], lb_ref[...])
    x1_ref[...] = x1
    h2_ref[...] = x1 * (1.0 + sc2_ref[0]) + sh2_ref[0]


def merge_ln1(att2d, ssd2d, proj2d, x2d, gate1, shift2, scale2, w_a, w_s, w_o, ln_g, ln_b,
              g_col0, row0):
    n, d = att2d.shape[0], x2d.shape[1]
    tm = 512
    blk0 = row0 // tm
    cst3 = lambda i: (0, 0, 0)
    cst = lambda i: (0, 0)
    kern = functools.partial(_merge_kernel, d=d)
    return pl.pallas_call(
        kern,
        grid=(n // tm,),
        in_specs=[pl.BlockSpec((tm, att2d.shape[1]), lambda i: (i, 0)),
                  pl.BlockSpec((tm, ssd2d.shape[1]), lambda i: (i + blk0, 0)),
                  pl.BlockSpec((tm, 2 * d), lambda i: (i + blk0, g_col0 // (2 * d))),
                  pl.BlockSpec((tm, d), lambda i: (i + blk0, 0)),
                  pl.BlockSpec((1, 1, d), cst3),
                  pl.BlockSpec((1, 1, d), cst3),
                  pl.BlockSpec((1, 1, d), cst3),
                  pl.BlockSpec(w_a.shape, cst),
                  pl.BlockSpec(w_s.shape, cst),
                  pl.BlockSpec(w_o.shape, cst),
                  pl.BlockSpec((1, d), cst),
                  pl.BlockSpec((1, d), cst)],
        out_specs=[pl.BlockSpec((tm, d), lambda i: (i, 0)),
                   pl.BlockSpec((tm, d), lambda i: (i, 0))],
        out_shape=[jax.ShapeDtypeStruct((n, d), F32),
                   jax.ShapeDtypeStruct((n, d), F32)],
        compiler_params=_cparams(("parallel",)),
        name="merge_ln1",
    )(att2d, ssd2d, proj2d, x2d, gate1, shift2, scale2, w_a, w_s, w_o, ln_g, ln_b)


ROUTE_T = 256


def _topk_rounds(scs, ids, k, payloads=None):
    n_p = len(scs)
    vals = [[] for _ in range(n_p)]
    sels = [[] for _ in range(n_p)]
    picks = [[] for _ in range(n_p)]
    scs = list(scs)
    for _ in range(k):
        ms = [jnp.max(sc, axis=0, keepdims=True) for sc in scs]
        ss = [jnp.min(jnp.where(scs[p] == ms[p], ids, float(1 << 24)), axis=0, keepdims=True)
              for p in range(n_p)]
        hits = [ids == ss[p] for p in range(n_p)]
        if payloads is not None:
            for p in range(n_p):
                picks[p].append(jnp.max(jnp.where(hits[p], payloads[p], -1.0), axis=0, keepdims=True))
        scs = [jnp.where(hits[p], -jnp.inf, scs[p]) for p in range(n_p)]
        for p in range(n_p):
            vals[p].append(ms[p])
            sels[p].append(ss[p])
    cat = lambda xs: jnp.concatenate(xs, axis=0)
    return [(cat(vals[p]), cat(sels[p]), cat(picks[p]) if payloads is not None else None)
            for p in range(n_p)]


STAIR_COUNT = (16, 8, 5, 4, 3, 2, 2, 2)
STAIR_ROWS = 16 + 8 * 7 + 8


def _stair_tables(t):
    K = PEER_TOPK
    r = lax.broadcasted_iota(jnp.int32, (STAIR_ROWS, t), 0)
    mid_a = 1 + ((r - K) >> 3)
    mid_b = (r - K) & 7
    a = jnp.where(r < K, 0, jnp.where(r < K + 56, mid_a, r - (K + 56) + 8))
    b = jnp.where(r < K, r, jnp.where(r < K + 56, mid_b, 0))
    count = jnp.full((STAIR_ROWS, t), 1, jnp.int32)
    for av, nb in enumerate(STAIR_COUNT):
        count = jnp.where(a == av, nb, count)
    valid = b < count
    pos_ids = (a * K + b).astype(F32)
    return valid, pos_ids


def _stair_candidates(v0, v1, i0, i1, valid):
    K = PEER_TOPK
    s_blocks = [v0[0:1, :] + v1]
    c_blocks = [i0[0:1, :] * float(PEER_N_KEYS) + i1]
    for a in range(1, 8):
        s_blocks.append(v0[a:a + 1, :] + v1[0:8, :])
        c_blocks.append(i0[a:a + 1, :] * float(PEER_N_KEYS) + i1[0:8, :])
    s_blocks.append(v0[8:K, :] + v1[0:1, :])
    c_blocks.append(i0[8:K, :] * float(PEER_N_KEYS) + i1[0:1, :])
    cand_s = jnp.where(valid, jnp.concatenate(s_blocks, axis=0), -jnp.inf)
    cand_i = jnp.concatenate(c_blocks, axis=0)
    return cand_s, cand_i


def _route_kernel(h2_ref, wq_ref, keys_ref, eidx_ref, gate_ref, q_ref, val_ref, idx_ref,
                  es_ref, gs_ref):
    t = ROUTE_T
    K = PEER_TOPK
    hb = h2_ref[...].astype(BF16)
    q = _dot(hb, wq_ref[...]).astype(BF16)
    for hj in range(2 * PEER_HEADS):
        q_ref[hj] = q[:, hj * PEER_HALF_DIM:(hj + 1) * PEER_HALF_DIM]

    key_ids = lax.broadcasted_iota(jnp.int32, (PEER_N_KEYS, t), 0).astype(F32)

    def level1(h, carry):
        scs = [_dot_nt(keys_ref[j], q_ref[2 * h + j]) for j in range(2)]
        res = _topk_rounds(scs, key_ids, K)
        for j in range(2):
            val_ref[2 * h + j] = res[j][0]
            idx_ref[2 * h + j] = res[j][1]
        return carry

    lax.fori_loop(0, PEER_HEADS, level1, 0)

    valid, pos_ids = _stair_tables(t)

    def level2(hp, carry):
        heads = [2 * hp, 2 * hp + 1]
        cands = [_stair_candidates(val_ref[2 * h], val_ref[2 * h + 1], idx_ref[2 * h], idx_ref[2 * h + 1], valid)
                 for h in heads]
        res = _topk_rounds([c[0] for c in cands], pos_ids, K, payloads=[c[1] for c in cands])
        for p, h in enumerate(heads):
            top_s = res[p][0]
            e = jnp.exp(top_s - top_s[0:1, :])
            gs_ref[h] = e / jnp.sum(e, axis=0, keepdims=True)
            es_ref[h] = res[p][2]
        return carry

    lax.fori_loop(0, PEER_HEADS // 2, level2, 0)

    eidx_ref[...] = es_ref[...].reshape(PEER_HEADS * K, t).T.astype(jnp.int32)
    gate_ref[...] = gs_ref[...].reshape(PEER_HEADS * K, t).T


def peer_route(h2, w_query, sub_keys):
    n, d = h2.shape
    t = ROUTE_T
    K = PEER_TOPK
    nsel = PEER_HEADS * K
    cst2 = lambda i: (0, 0)
    return pl.pallas_call(
        _route_kernel,
        grid=(n // t,),
        in_specs=[pl.BlockSpec((t, d), lambda i: (i, 0)),
                  pl.BlockSpec(w_query.shape, cst2),
                  pl.BlockSpec(sub_keys.shape, lambda i: (0, 0, 0))],
        out_specs=[pl.BlockSpec((t, nsel), lambda i: (i, 0)),
                   pl.BlockSpec((t, nsel), lambda i: (i, 0))],
        out_shape=[jax.ShapeDtypeStruct((n, nsel), jnp.int32),
                   jax.ShapeDtypeStruct((n, nsel), F32)],
        scratch_shapes=[pltpu.VMEM((2 * PEER_HEADS, t, PEER_HALF_DIM), BF16),
                        pltpu.VMEM((2 * PEER_HEADS, K, t), F32),
                        pltpu.VMEM((2 * PEER_HEADS, K, t), F32),
                        pltpu.VMEM((PEER_HEADS, K, t), F32),
                        pltpu.VMEM((PEER_HEADS, K, t), F32)],
        compiler_params=_cparams(("parallel",)),
        name="peer_route",
    )(h2, w_query, sub_keys)


SC_CORES = 2
SC_SUBCORES = 16
SC_WORKERS = SC_CORES * SC_SUBCORES
SC_LANES = 16
GATHER_ROWS = 32
GATHER_BUFS = 4
TOKEN_BLOCK = 8
COMBINE_COLS = 2
DOT_ROWS = 16
PARTS_PER_ROW = 128 // SC_LANES


def pack_table(tab):
    e, d = tab.shape
    bits = lax.bitcast_convert_type(tab, jnp.uint32).reshape(e, d // (2 * SC_LANES), 2, SC_LANES)
    lo_src = lax.bitcast_convert_type(bits[:, :, 0, :], F32)
    low = lax.bitcast_convert_type(lo_src.astype(BF16), jnp.uint16).astype(jnp.int32)
    hi_bits = bits[:, :, 1, :]
    sign = hi_bits & jnp.uint32(0x80000000)
    mag = (hi_bits & jnp.uint32(0x7FFFFFFF)).astype(jnp.int32)
    high = jnp.maximum(mag + 0x8000 - low, 0) >> 16
    word = sign | (high.astype(jnp.uint32) << 16) | low.astype(jnp.uint32)
    return lax.bitcast_convert_type(word, jnp.int32).reshape(e, d // 2)


def _unpack_pair(w):
    lo = lax.bitcast_convert_type(jnp.left_shift(w, 16), F32)
    hi = lax.bitcast_convert_type(w, F32)
    return lo, hi


def _sc_mesh():
    return plsc.VectorSubcoreMesh(core_axis_name="c", subcore_axis_name="s")


def _worker_id():
    return lax.axis_index("s") * SC_CORES + lax.axis_index("c")


def _gather_pipeline(tab_hbm, idx_blk, rows_v, sems, n_chunks, chunks_per_tok, compute):
    ahead = GATHER_BUFS - 1

    def gather(q, buf):
        tok = q // chunks_per_tok
        ch = q % chunks_per_tok
        return pltpu.make_async_copy(
            tab_hbm.at[idx_blk.at[tok, pl.ds(ch * GATHER_ROWS, GATHER_ROWS)]],
            rows_v.at[buf], sems.at[buf])

    for q in range(ahead):
        gather(q, q).start()

    def ring(i, carry):
        for u in range(GATHER_BUFS):
            q = GATHER_BUFS * i + u

            @pl.when(q + ahead < n_chunks)
            def _():
                gather(q + ahead, (u + ahead) % GATHER_BUFS).start()

            gather(q, u).wait()
            compute(q, u)
        return carry

    lax.fori_loop(0, n_chunks // GATHER_BUFS, ring, 0)


def _block_pipeline(n_blocks, in_copies, out_copy, work):
    for cp in in_copies(0, 0):
        cp.start()

    def block(bi, carry):
        slot = bi % 2
        for cp in in_copies(bi, slot):
            cp.wait()

        @pl.when(bi + 1 < n_blocks)
        def _():
            for cp in in_copies(bi + 1, 1 - slot):
                cp.start()

        @pl.when(bi >= 2)
        def _():
            out_copy(bi - 2, slot).wait()

        work(bi, slot)
        out_copy(bi, slot).start()
        return carry

    lax.fori_loop(0, n_blocks, block, 0)
    out_copy(n_blocks - 2, n_blocks % 2).wait()
    out_copy(n_blocks - 1, (n_blocks - 1) % 2).wait()


def _dots_phase(u_hbm, h2_hbm, eidx_hbm, out_hbm, idx_v, x_v, rows_v, o_v, sems, in_sems, out_sems,
                n, d, nsel):
    tok_per_w = n // SC_WORKERS
    chunks_per_tok = nsel // GATHER_ROWS
    n_chunks = TOKEN_BLOCK * chunks_per_tok
    rows_per_chunk = GATHER_ROWS // PARTS_PER_ROW
    out_rows = TOKEN_BLOCK * nsel // PARTS_PER_ROW
    base = _worker_id() * tok_per_w

    def in_copies(bi, slot):
        t0 = base + bi * TOKEN_BLOCK
        return [pltpu.make_async_copy(eidx_hbm.at[pl.ds(t0, TOKEN_BLOCK)], idx_v.at[slot], in_sems.at[slot]),
                pltpu.make_async_copy(h2_hbm.at[pl.ds(t0, TOKEN_BLOCK)], x_v.at[slot], in_sems.at[slot])]

    def out_copy(bi, slot):
        t0 = base + bi * TOKEN_BLOCK
        return pltpu.make_async_copy(o_v.at[slot], out_hbm.at[pl.ds(t0 * (nsel // PARTS_PER_ROW), out_rows)],
                                     out_sems.at[slot])

    def work(bi, slot):
        def compute(q, buf):
            tok = q // chunks_per_tok
            zero = jnp.zeros((SC_LANES,), F32)
            for r0 in range(0, GATHER_ROWS, DOT_ROWS):
                def cbody(c, accs, r0=r0):
                    x_lo = x_v[slot, tok, pl.ds(c * 2 * SC_LANES, SC_LANES)]
                    x_hi = x_v[slot, tok, pl.ds(c * 2 * SC_LANES + SC_LANES, SC_LANES)]
                    out = []
                    for r in range(DOT_ROWS):
                        lo, hi = _unpack_pair(rows_v[buf, r0 + r, pl.ds(c * SC_LANES, SC_LANES)])
                        out.append(accs[r] + lo * x_lo + hi * x_hi)
                    return tuple(out)

                accs = lax.fori_loop(0, d // (2 * SC_LANES), cbody, tuple(zero for _ in range(DOT_ROWS)))
                for r in range(DOT_ROWS):
                    rr = r0 + r
                    o_v[slot, q * rows_per_chunk + rr // PARTS_PER_ROW,
                        pl.ds((rr % PARTS_PER_ROW) * SC_LANES, SC_LANES)] = accs[r]

        _gather_pipeline(u_hbm, idx_v.at[slot], rows_v, sems, n_chunks, chunks_per_tok, compute)

    _block_pipeline(tok_per_w // TOKEN_BLOCK, in_copies, out_copy, work)


def _combine_phase(v_hbm, w_hbm, eidx_hbm, out_hbm, idx_v, w_v, rows_v, y_v, sems, in_sems, out_sems,
                   n, d, nsel):
    tok_per_w = n // SC_WORKERS
    chunks_per_tok = nsel // GATHER_ROWS
    n_chunks = TOKEN_BLOCK * chunks_per_tok
    base = _worker_id() * tok_per_w

    def in_copies(bi, slot):
        t0 = base + bi * TOKEN_BLOCK
        return [pltpu.make_async_copy(eidx_hbm.at[pl.ds(t0, TOKEN_BLOCK)], idx_v.at[slot], in_sems.at[slot]),
                pltpu.make_async_copy(w_hbm.at[pl.ds(t0, TOKEN_BLOCK)], w_v.at[slot], in_sems.at[slot])]

    def out_copy(bi, slot):
        t0 = base + bi * TOKEN_BLOCK
        return pltpu.make_async_copy(y_v.at[slot], out_hbm.at[pl.ds(t0, TOKEN_BLOCK)], out_sems.at[slot])

    def work(bi, slot):
        slot_vec = jnp.full((SC_LANES,), slot, jnp.int32)

        def compute(q, buf):
            tok = q // chunks_per_tok
            ch = q % chunks_per_tok
            tok_vec = jnp.full((SC_LANES,), tok, jnp.int32)
            ws = [plsc.load_gather(
                w_v, [slot_vec, tok_vec, jnp.full((SC_LANES,), ch * GATHER_ROWS + r, jnp.int32)])
                for r in range(GATHER_ROWS)]
            first = ch == 0

            def cbody(cg, carry):
                cols = [cg * COMBINE_COLS + u for u in range(COMBINE_COLS)]
                sl_lo = [pl.ds(c * 2 * SC_LANES, SC_LANES) for c in cols]
                sl_hi = [pl.ds(c * 2 * SC_LANES + SC_LANES, SC_LANES) for c in cols]
                zero = jnp.zeros((SC_LANES,), F32)
                a_lo = [jnp.where(first, zero, y_v[slot, tok, s]) for s in sl_lo]
                a_hi = [jnp.where(first, zero, y_v[slot, tok, s]) for s in sl_hi]
                for r in range(GATHER_ROWS):
                    for u in range(COMBINE_COLS):
                        lo, hi = _unpack_pair(rows_v[buf, r, pl.ds(cols[u] * SC_LANES, SC_LANES)])
                        a_lo[u] = a_lo[u] + ws[r] * lo
                        a_hi[u] = a_hi[u] + ws[r] * hi
                for u in range(COMBINE_COLS):
                    y_v[slot, tok, sl_lo[u]] = a_lo[u]
                    y_v[slot, tok, sl_hi[u]] = a_hi[u]
                return carry

            lax.fori_loop(0, d // (2 * SC_LANES) // COMBINE_COLS, cbody, 0)

        _gather_pipeline(v_hbm, idx_v.at[slot], rows_v, sems, n_chunks, chunks_per_tok, compute)

    _block_pipeline(tok_per_w // TOKEN_BLOCK, in_copies, out_copy, work)


def peer_experts_sc(u_pk=None, h2=None, eidx=None, v_pk=None, wts_prev=None, eidx_prev=None):
    do_dots = u_pk is not None
    do_comb = v_pk is not None
    n_c = wts_prev.shape[0] if do_comb else 0
    n_d = h2.shape[0] if do_dots else 0
    nsel = (eidx if do_dots else eidx_prev).shape[1]
    d = 2 * (u_pk if do_dots else v_pk).shape[1]
    out_types, scratch, args = [], [], []
    if do_comb:
        out_types.append(jax.ShapeDtypeStruct((n_c, d), F32))
        args += [v_pk, wts_prev, eidx_prev]
        scratch += [pltpu.VMEM((2, TOKEN_BLOCK, nsel), F32)]
    if do_dots:
        out_types.append(jax.ShapeDtypeStruct((n_d * nsel // PARTS_PER_ROW, 128), F32))
        args += [u_pk, h2, eidx]
        scratch += [pltpu.VMEM((2, TOKEN_BLOCK * nsel // PARTS_PER_ROW, 128), F32)]
    scratch += [pltpu.VMEM((2, TOKEN_BLOCK, d), F32),
                pltpu.VMEM((2, TOKEN_BLOCK, nsel), jnp.int32),
                pltpu.VMEM((GATHER_BUFS, GATHER_ROWS, d // 2), jnp.int32),
                pltpu.SemaphoreType.DMA((GATHER_BUFS,)),
                pltpu.SemaphoreType.DMA((2,)),
                pltpu.SemaphoreType.DMA((2,))]

    def body(*refs):
        refs = list(refs)
        ins = refs[:len(args)]
        outs = refs[len(args):len(args) + len(out_types)]
        scr = refs[len(args) + len(out_types):]
        xy_v, idx_v, rows_v, sems, in_sems, out_sems = scr[-6:]
        if do_comb:
            v_hbm, w_hbm, ep_hbm = ins[:3]
            _combine_phase(v_hbm, w_hbm, ep_hbm, outs[0], idx_v, scr[0], rows_v, xy_v,
                           sems, in_sems, out_sems, n_c, d, nsel)
        if do_dots:
            u_hbm, h2_hbm, e_hbm = ins[-3:]
            _dots_phase(u_hbm, h2_hbm, e_hbm, outs[-1], idx_v, xy_v, rows_v, scr[-7],
                        sems, in_sems, out_sems, n_d, d, nsel)

    res = pl.kernel(
        body, out_type=tuple(out_types), mesh=_sc_mesh(), scratch_types=scratch,
        compiler_params=pltpu.CompilerParams(needs_layout_passes=False),
        name="peer_experts_sc" + ("_c" if do_comb else "") + ("_d" if do_dots else ""),
    )(*args)
    res = list(res)
    y_prev = res[0] if do_comb else None
    part = res[-1] if do_dots else None
    return y_prev, part


def _weights_kernel(part_ref, gate_ref, red_ref, o_ref):
    act = jnp.dot(part_ref[...], red_ref[...], preferred_element_type=F32,
                  precision=lax.Precision.HIGHEST)
    gelu = 0.5 * act * (1.0 + lax.erf(act * (2.0 ** -0.5)))
    o_ref[...] = gate_ref[...] * gelu


def peer_weights(part2d, gate):
    n, nsel = gate.shape
    tm = 512
    w = part2d.shape[1]
    red = (jnp.arange(w)[:, None] // SC_LANES == jnp.arange(nsel)[None, :]).astype(F32)
    return pl.pallas_call(
        _weights_kernel,
        grid=(n // tm,),
        in_specs=[pl.BlockSpec((tm, w), lambda i: (i, 0)),
                  pl.BlockSpec((tm, nsel), lambda i: (i, 0)),
                  pl.BlockSpec((w, nsel), lambda i: (0, 0))],
        out_specs=pl.BlockSpec((tm, nsel), lambda i: (i, 0)),
        out_shape=jax.ShapeDtypeStruct((n, nsel), F32),
        compiler_params=_cparams(("parallel",)),
        name="peer_weights",
    )(part2d, gate, red)


def _final_kernel(x1_ref, y_ref, gate_ref, lg_ref, lb_ref, o_ref):
    o_ref[...] = _layer_norm(ALPHA * x1_ref[...] + gate_ref[0] * y_ref[...], lg_ref[...], lb_ref[...])


def final_ln(x1, y_ffn, gate2, ln_g, ln_b):
    n, d = x1.shape
    tm = 1024
    return pl.pallas_call(
        _final_kernel,
        grid=(n // tm,),
        in_specs=[pl.BlockSpec((tm, d), lambda i: (i, 0)),
                  pl.BlockSpec((tm, d), lambda i: (i, 0)),
                  pl.BlockSpec((1, 1, d), lambda i: (0, 0, 0)),
                  pl.BlockSpec((1, d), lambda i: (0, 0)),
                  pl.BlockSpec((1, d), lambda i: (0, 0))],
        out_specs=pl.BlockSpec((tm, d), lambda i: (i, 0)),
        out_shape=jax.ShapeDtypeStruct((n, d), F32),
        compiler_params=_cparams(("parallel",)),
        name="final_ln",
    )(x1, y_ffn, gate2, ln_g, ln_b)


def _cumsum(widths):
    out, t = [], 0
    for w in widths:
        t += w
        out.append(t)
    return out


def _block(x, c, w_ada, b_ada, w_in, conv_w, conv_b, dt_bias, a_log, d_skip, ssd_norm_w,
           lambda_q1, lambda_k1, lambda_q2, lambda_k2, da_subln_w, w_attn_branch, w_ssd_branch,
           w_out, ln1_g, ln1_b, peer_w_query, peer_sub_keys, peer_u, peer_v, ln2_g, ln2_b,
           layer):
    bsz, s, d = x.shape
    qk_w = DA_HEADS * 2 * DA_HEAD_DIM
    v_w = DA_HEADS * DA_V_DIM
    d_inner = w_ssd_branch.shape[0]
    n_heads = d_inner // SSD_HEAD_DIM
    bc_w = SSD_GROUPS * SSD_D_STATE
    xbc_w = d_inner + 2 * bc_w

    c_pad = jnp.pad(c, ((0, 8 - bsz), (0, 0)))
    mod = ada_mod(c_pad, w_ada, b_ada)[:bsz]
    shift1, scale1, gate1, shift2, scale2, gate2 = [m.reshape(bsz, 1, d) for m in jnp.split(mod, 6, axis=-1)]

    splits = _cumsum([qk_w, qk_w, v_w, d_inner, xbc_w, n_heads, 2 * d])
    wq, wk, wv, wz, wxbc, wdt, wg = [w_in[:, a:b] for a, b in zip([0] + splits[:-1], splits)]
    wq = wq * ((DA_HEAD_DIM ** -0.5) * LOG2E)
    w_main = jnp.concatenate([wxbc, wq, wz, wg, wk, wv], axis=1).astype(BF16)
    w_dt = jnp.pad(wdt, ((0, 0), (0, 128 - n_heads))).astype(BF16)
    xbc_c0 = 0
    q_c0 = xbc_w
    z_c0 = q_c0 + qk_w
    g_c0 = z_c0 + d_inner
    k_c0 = g_c0 + 2 * d
    v_c0 = k_c0 + qk_w
    lambda_init = 0.8 - 0.6 * math.exp(-0.3 * layer)
    lam_params = jnp.stack([lambda_q1, lambda_k1, lambda_q2, lambda_k2]).astype(F32)
    pad_h = lambda t: jnp.pad(t.astype(F32), (0, 128 - n_heads)).reshape(1, 128)
    dtb_p, alog_p = pad_h(dt_bias), pad_h(a_log)
    dskip_e = jnp.repeat(d_skip.astype(F32), SSD_HEAD_DIM).reshape(1, d_inner)
    w_ab, w_sb, w_o = w_attn_branch.astype(BF16), w_ssd_branch.astype(BF16), w_out.astype(BF16)
    w_qp, keys_b = peer_w_query.astype(BF16), peer_sub_keys.astype(BF16)
    u_pk, v_pk = pack_table(peer_u), pack_table(peer_v)

    outs = []
    pending = None

    def retire(cur):
        nonlocal pending
        kw = {}
        if pending is not None:
            p_x1, p_gate2, p_eidx, p_gate, p_part = pending
            wts = peer_weights(p_part.reshape(p_gate.shape[0], -1), p_gate)
            kw.update(v_pk=v_pk, wts_prev=wts, eidx_prev=p_eidx)
        if cur is not None:
            kw.update(u_pk=u_pk, h2=cur[4], eidx=cur[2])
        y_prev, part = peer_experts_sc(**kw)
        if pending is not None:
            outs.append(final_ln(p_x1, y_prev, p_gate2, ln2_g.reshape(1, d), ln2_b.reshape(1, d)))
        pending = None if cur is None else (cur[0], cur[1], cur[2], cur[3], part)

    for b in range(bsz):
        sl = slice(b, b + 1)
        xb = x[b]
        proj, dt_raw = in_proj(xb, shift1[sl], scale1[sl], w_main, w_dt, s)
        proj3 = proj.reshape(1, s, proj.shape[1])
        y_ssd = ssd_branch(proj3, dt_raw.reshape(1, s, 128), conv_w, conv_b.reshape(1, xbc_w),
                           dtb_p, alog_p, dskip_e, ssd_norm_w.reshape(1, d_inner),
                           xbc_c0, z_c0, d_inner, n_heads).reshape(s, d_inner)
        row0 = 0
        for gs in _group_sizes(s, first=(b == 0), last=(b == bsz - 1)):
            y_attn = diff_attention(proj3, lam_params, da_subln_w.reshape(1, DA_V_DIM), q_c0, k_c0, v_c0,
                                    lambda_init, row0 // ATT_TQ, gs // ATT_TQ)
            x1, h2 = merge_ln1(y_attn.reshape(gs, v_w), y_ssd, proj, xb,
                               gate1[sl], shift2[sl], scale2[sl], w_ab, w_sb, w_o,
                               ln1_g.reshape(1, d), ln1_b.reshape(1, d), g_c0, row0)
            eidx, gate = peer_route(h2, w_qp, keys_b)
            retire((x1, gate2[sl], eidx, gate, h2))
            row0 += gs
    retire(None)
    return jnp.concatenate(outs, axis=0).reshape(bsz, s, d)


def _group_sizes(s, first, last):
    e = s // 8
    if first and last:
        return [e, 3 * e, 2 * e, e, e]
    if first:
        return [e, 3 * e, 4 * e]
    if last:
        return [4 * e, 2 * e, e, e]
    return [4 * e, 4 * e]


def kernel(x, c, w_ada, b_ada, w_in, conv_w, conv_b, dt_bias, a_log, d_skip, ssd_norm_w,
           lambda_q1, lambda_k1, lambda_q2, lambda_k2, da_subln_w, w_attn_branch, w_ssd_branch,
           w_out, ln1_g, ln1_b, peer_w_query, peer_sub_keys, peer_u, peer_v, ln2_g, ln2_b):
    for l in range(w_ada.shape[0]):
        x = _block(x, c, w_ada[l], b_ada[l], w_in[l], conv_w[l], conv_b[l], dt_bias[l], a_log[l],
                   d_skip[l], ssd_norm_w[l], lambda_q1[l], lambda_k1[l], lambda_q2[l], lambda_k2[l],
                   da_subln_w[l], w_attn_branch[l], w_ssd_branch[l], w_out[l], ln1_g[l], ln1_b[l],
                   peer_w_query[l], peer_sub_keys[l], peer_u[l], peer_v[l], ln2_g[l], ln2_b[l], l)
    return x
```

```python
import functools
import math

import jax
import jax.numpy as jnp
from jax import lax
from jax.experimental import pallas as pl
from jax.experimental.pallas import tpu as pltpu
from jax.experimental.pallas import tpu_sc as plsc

F32 = jnp.float32
BF16 = jnp.bfloat16

DA_HEADS = 8
DA_HEAD_DIM = 64
DA_V_DIM = 2 * DA_HEAD_DIM
SSD_HEAD_DIM = 64
SSD_GROUPS = 4
SSD_D_STATE = 128
SSD_CONV = 4
SSD_CHUNK = 256
PEER_N_KEYS = 128
PEER_HEADS = 8
PEER_TOPK = 16
PEER_HALF_DIM = 128
DEPTH = 1
ALPHA = (2 * DEPTH) ** 0.25
EPS = 1e-5
LOG2E = 1.4426950408889634
NEG_BIG = -1e30

VMEM_LIMIT_BYTES = 56 * 1024 * 1024


def _cparams(sem):
    return pltpu.CompilerParams(dimension_semantics=sem, vmem_limit_bytes=VMEM_LIMIT_BYTES)


def _dot(a, b):
    return jnp.dot(a, b, preferred_element_type=F32)


def _dot_nt(a, b):
    return lax.dot_general(a, b, (((1,), (1,)), ((), ())), preferred_element_type=F32)


def _ada_kernel(c_ref, w_ref, b_ref, o_ref):
    c = c_ref[...]
    sc = c * jax.nn.sigmoid(c)
    o_ref[...] = jnp.dot(sc, w_ref[...], preferred_element_type=F32,
                         precision=lax.Precision.HIGHEST) + b_ref[...]


def ada_mod(c_pad, w_ada, b_ada):
    d, n = w_ada.shape
    tn = 512
    return pl.pallas_call(
        _ada_kernel,
        grid=(n // tn,),
        in_specs=[pl.BlockSpec((c_pad.shape[0], d), lambda j: (0, 0)),
                  pl.BlockSpec((d, tn), lambda j: (0, j)),
                  pl.BlockSpec((1, tn), lambda j: (0, j))],
        out_specs=pl.BlockSpec((c_pad.shape[0], tn), lambda j: (0, j)),
        out_shape=jax.ShapeDtypeStruct((c_pad.shape[0], n), F32),
        compiler_params=_cparams(("arbitrary",)),
        name="ada_mod",
    )(c_pad, w_ada, b_ada.reshape(1, n))


def _inproj_kernel(x_ref, sh_ref, sc_ref, w_ref, wdt_ref, o_ref, dt_ref, h_ref):
    @pl.when(pl.program_id(1) == 0)
    def _():
        h = x_ref[...] * (1.0 + sc_ref[0]) + sh_ref[0]
        hb = h.astype(BF16)
        h_ref[...] = hb
        dt_ref[...] = _dot(hb, wdt_ref[...])

    o_ref[...] = _dot(h_ref[...], w_ref[...]).astype(BF16)


def in_proj(x2d, shift1, scale1, w_main, w_dt, seq):
    n, d = x2d.shape
    tm, tn = 1024, 1024
    tiles_per_batch = seq // tm
    width = w_main.shape[1]
    bmap = lambda i, j: (i // tiles_per_batch, 0, 0)
    return pl.pallas_call(
        _inproj_kernel,
        grid=(n // tm, width // tn),
        in_specs=[pl.BlockSpec((tm, d), lambda i, j: (i, 0)),
                  pl.BlockSpec((1, 1, d), bmap),
                  pl.BlockSpec((1, 1, d), bmap),
                  pl.BlockSpec((d, tn), lambda i, j: (0, j)),
                  pl.BlockSpec((d, 128), lambda i, j: (0, 0))],
        out_specs=[pl.BlockSpec((tm, tn), lambda i, j: (i, j)),
                   pl.BlockSpec((tm, 128), lambda i, j: (i, 0))],
        out_shape=[jax.ShapeDtypeStruct((n, width), BF16),
                   jax.ShapeDtypeStruct((n, 128), F32)],
        scratch_shapes=[pltpu.VMEM((tm, d), BF16)],
        compiler_params=_cparams(("parallel", "arbitrary")),
        name="in_proj",
    )(x2d, shift1, scale1, w_main, w_dt)


ATT_TQ = 256
ATT_TK = 512


def _attn_kernel(q_ref, k_ref, v_ref, lam_ref, w_ref, o_ref,
                 vt_ref, sa_ref, sb_ref, bias_ref, m_ref, l_ref, acc_ref, *, kv_len, q_block0, lambda_init):
    tq, tk = ATT_TQ, ATT_TK
    h = pl.program_id(1)
    qi = pl.program_id(2) + q_block0
    n_kb_total = kv_len // tk

    @pl.when(pl.program_id(2) == 0)
    def _():
        for c in range(n_kb_total):
            blk = v_ref[0, c * tk:(c + 1) * tk, :].astype(F32)
            vt_ref[c] = blk.T.astype(BF16)

    slope2 = jnp.exp2(-(h + 1).astype(F32)) * LOG2E
    i0 = qi * tq
    kb_diag = i0 // tk
    q = q_ref[0]
    qs = [q[:, c * DA_HEAD_DIM:(c + 1) * DA_HEAD_DIM] for c in range(2)]
    rows = lax.broadcasted_iota(jnp.int32, (tk, tq), 0)
    cols = lax.broadcasted_iota(jnp.int32, (tk, tq), 1)
    bias0 = rows.astype(F32) * slope2
    bias_ref[0] = bias0
    bias_ref[1] = jnp.where(rows + kb_diag * tk <= cols + i0, bias0, NEG_BIG)

    m_ref[...] = jnp.full(m_ref.shape, NEG_BIG, F32)
    l_ref[...] = jnp.zeros(l_ref.shape, F32)
    acc_ref[...] = jnp.zeros(acc_ref.shape, F32)

    def scores(kb, s_ref):
        kblk = k_ref[0, pl.ds(pl.multiple_of(kb * tk, tk), tk), :]
        for c in range(2):
            s_ref[c] = _dot_nt(kblk[:, c * DA_HEAD_DIM:(c + 1) * DA_HEAD_DIM], qs[c])

    def softmax_pv(kb, s_ref):
        vt = vt_ref[kb]
        cb = slope2 * (kb * tk - i0).astype(F32)
        bias = bias_ref[(kb == kb_diag).astype(jnp.int32)]
        for c in range(2):
            s = s_ref[c] + bias
            m_old = m_ref[c]
            m_new = jnp.maximum(m_old, jnp.max(s, axis=0, keepdims=True) + cb)
            p = jnp.exp2(s - (m_new - cb))
            alpha = jnp.exp2(m_old - m_new)
            l_ref[c] = alpha * l_ref[c] + jnp.sum(p, axis=0, keepdims=True)
            acc_ref[c] = alpha * acc_ref[c] + _dot(vt, p.astype(BF16))
            m_ref[c] = m_new

    n_blocks = kb_diag + 1
    scores(0, sa_ref)

    def body(i, carry):
        kb = 2 * i
        scores(kb + 1, sb_ref)
        softmax_pv(kb, sa_ref)
        scores(jnp.minimum(kb + 2, kb_diag), sa_ref)
        softmax_pv(kb + 1, sb_ref)
        return carry

    lax.fori_loop(0, n_blocks // 2, body, 0)

    @pl.when(n_blocks % 2 == 1)
    def _():
        softmax_pv(kb_diag, sa_ref)

    lam_p = lam_ref[...]
    lam = (jnp.exp(jnp.sum(lam_p[0:1] * lam_p[1:2])) - jnp.exp(jnp.sum(lam_p[2:3] * lam_p[3:4]))
           + lambda_init)
    o_t = acc_ref[0] / l_ref[0] - lam * (acc_ref[1] / l_ref[1])
    o = o_t.T
    o = o * lax.rsqrt(jnp.mean(o * o, axis=-1, keepdims=True) + EPS) * w_ref[...]
    o_ref[0] = (o * (1.0 - lambda_init)).astype(BF16)


def diff_attention(proj3, lam_params, subln_w, q_col0, k_col0, v_col0, lambda_init, q_block0, n_q_blocks):
    b, s, _ = proj3.shape
    tq, tk = ATT_TQ, ATT_TK
    kv_len = -(-((q_block0 + n_q_blocks) * tq) // tk) * tk
    qb, kb, vb = q_col0 // 128, k_col0 // 128, v_col0 // 128
    kern = functools.partial(_attn_kernel, kv_len=kv_len, q_block0=q_block0, lambda_init=lambda_init)
    return pl.pallas_call(
        kern,
        grid=(b, DA_HEADS, n_q_blocks),
        in_specs=[pl.BlockSpec((1, tq, 128), lambda bi, h, qi: (bi, qi + q_block0, qb + h)),
                  pl.BlockSpec((1, kv_len, 128), lambda bi, h, qi: (bi, 0, kb + h)),
                  pl.BlockSpec((1, kv_len, 128), lambda bi, h, qi: (bi, 0, vb + h)),
                  pl.BlockSpec((4, DA_HEAD_DIM), lambda bi, h, qi: (0, 0)),
                  pl.BlockSpec((1, DA_V_DIM), lambda bi, h, qi: (0, 0))],
        out_specs=pl.BlockSpec((1, tq, 128), lambda bi, h, qi: (bi, qi, h)),
        out_shape=jax.ShapeDtypeStruct((b, n_q_blocks * tq, DA_HEADS * DA_V_DIM), BF16),
        scratch_shapes=[pltpu.VMEM((kv_len // tk, DA_V_DIM, tk), BF16),
                        pltpu.VMEM((2, tk, tq), F32),
                        pltpu.VMEM((2, tk, tq), F32),
                        pltpu.VMEM((2, tk, tq), F32),
                        pltpu.VMEM((2, 1, tq), F32),
                        pltpu.VMEM((2, 1, tq), F32),
                        pltpu.VMEM((2, DA_V_DIM, tq), F32)],
        compiler_params=_cparams(("parallel", "arbitrary", "arbitrary")),
        name="diff_attn",
    )(proj3, proj3, proj3, lam_params, subln_w)


def _split3(x):
    hi = x.astype(BF16)
    r1 = x - hi.astype(F32)
    mid = r1.astype(BF16)
    lo = (r1 - mid.astype(F32)).astype(BF16)
    return hi, mid, lo


def _ssd_kernel(xbc_ref, z_ref, dt_ref, cw_ref, cb_ref, dtb_ref, alog_ref, dsk_ref, nw_ref,
                o_ref, ext_ref, state_ref, y_ref, *, d_inner, n_heads):
    L = SSD_CHUNK
    P = SSD_HEAD_DIM
    NS = SSD_D_STATE
    G = SSD_GROUPS
    R = n_heads // G
    bc_w = G * NS
    ci = pl.program_id(1)

    @pl.when(ci == 0)
    def _():
        state_ref[...] = jnp.zeros(state_ref.shape, F32)
        ext_ref[0:8, :] = jnp.zeros((8, ext_ref.shape[1]), F32)

    ext_ref[8:8 + L, :] = xbc_ref[0].astype(F32)
    conv = cb_ref[...]
    for k in range(SSD_CONV):
        off = 8 - (SSD_CONV - 1) + k
        conv = conv + cw_ref[k:k + 1, :] * ext_ref[off:off + L, :]
    ext_ref[0:8, :] = ext_ref[L:L + 8, :]
    xc = conv * jax.nn.sigmoid(conv)

    dtr = dt_ref[0] + dtb_ref[...]
    dt = jnp.maximum(dtr, 0.0) + jnp.log1p(jnp.exp(-jnp.abs(dtr)))
    a = -jnp.exp(alog_ref[...])
    da = dt * a
    tri = (lax.broadcasted_iota(jnp.int32, (L, L), 0) >= lax.broadcasted_iota(jnp.int32, (L, L), 1))
    tri_b = tri.astype(BF16)
    hi, mid, lo = _split3(da)
    cum = _dot(tri_b, hi) + _dot(tri_b, mid) + _dot(tri_b, lo)
    cum_t = cum.T
    dt_t = dt.T
    cum_last = cum[L - 1:L, :]
    e_cum = jnp.exp(cum)
    dec_end = jnp.exp(cum_last - cum) * dt
    e_last = jnp.exp(cum_last)

    for g in range(G):
        bg = xc[:, d_inner + g * NS:d_inner + (g + 1) * NS]
        cg = xc[:, d_inner + bc_w + g * NS:d_inner + bc_w + (g + 1) * NS]
        bg_b = bg.astype(BF16)
        cg_b = cg.astype(BF16)
        bgt_b = bg.T.astype(BF16)
        cbm = _dot_nt(cg_b, bg_b)
        st_g = state_ref[g]
        y_state = _dot(cg_b, st_g.astype(BF16))
        for r in range(R):
            hh = g * R + r
            xh = xc[:, hh * P:(hh + 1) * P]
            seg = cum[:, hh:hh + 1] - cum_t[hh:hh + 1, :]
            w = jnp.where(tri, jnp.exp(seg), 0.0) * cbm * dt_t[hh:hh + 1, :]
            yh = _dot(w.astype(BF16), xh.astype(BF16))
            yh = yh + y_state[:, r * P:(r + 1) * P] * e_cum[:, hh:hh + 1]
            y_ref[:, hh * P:(hh + 1) * P] = yh
            dx = (xh * dec_end[:, hh:hh + 1]).astype(BF16)
            state_ref[g, :, r * P:(r + 1) * P] = (
                st_g[:, r * P:(r + 1) * P] * e_last[:, hh:hh + 1] + _dot(bgt_b, dx))

    xs = xc[:, :d_inner]
    z = z_ref[0].astype(F32)
    y = (y_ref[...] + dsk_ref[...] * xs) * (z * jax.nn.sigmoid(z))
    gw = d_inner // G
    for g in range(G):
        yg = y[:, g * gw:(g + 1) * gw]
        yn = yg * lax.rsqrt(jnp.mean(yg * yg, axis=-1, keepdims=True) + EPS)
        o_ref[0, :, g * gw:(g + 1) * gw] = (yn * nw_ref[:, g * gw:(g + 1) * gw]).astype(BF16)


def ssd_branch(proj3, dt3, conv_w, conv_b, dt_bias_p, a_log_p, dskip_e, norm_w,
               xbc_col0, z_col0, d_inner, n_heads):
    b, s, _ = proj3.shape
    L = SSD_CHUNK
    xw = conv_w.shape[1]
    kern = functools.partial(_ssd_kernel, d_inner=d_inner, n_heads=n_heads)
    cst = lambda bi, ci: (0, 0)
    return pl.pallas_call(
        kern,
        grid=(b, s // L),
        in_specs=[pl.BlockSpec((1, L, xw), lambda bi, ci: (bi, ci, xbc_col0 // xw)),
                  pl.BlockSpec((1, L, d_inner), lambda bi, ci: (bi, ci, z_col0 // d_inner)),
                  pl.BlockSpec((1, L, 128), lambda bi, ci: (bi, ci, 0)),
                  pl.BlockSpec((SSD_CONV, xw), cst),
                  pl.BlockSpec((1, xw), cst),
                  pl.BlockSpec((1, 128), cst),
                  pl.BlockSpec((1, 128), cst),
                  pl.BlockSpec((1, d_inner), cst),
                  pl.BlockSpec((1, d_inner), cst)],
        out_specs=pl.BlockSpec((1, L, d_inner), lambda bi, ci: (bi, ci, 0)),
        out_shape=jax.ShapeDtypeStruct((b, s, d_inner), BF16),
        scratch_shapes=[pltpu.VMEM((L + 8, xw), F32),
                        pltpu.VMEM((SSD_GROUPS, SSD_D_STATE, d_inner // SSD_GROUPS), F32),
                        pltpu.VMEM((L, d_inner), F32)],
        compiler_params=_cparams(("parallel", "arbitrary")),
        name="ssd_scan",
    )(proj3, proj3, dt3, conv_w, conv_b, dt_bias_p, a_log_p, dskip_e, norm_w)


def _layer_norm(v, g, b):
    mu = jnp.mean(v, axis=-1, keepdims=True)
    d = v - mu
    var = jnp.mean(d * d, axis=-1, keepdims=True)
    return d * lax.rsqrt(var + EPS) * g + b


def _merge_kernel(att_ref, ssd_ref, g_ref, x_ref, gate_ref, sh2_ref, sc2_ref,
                  wa_ref, ws_ref, wo_ref, lg_ref, lb_ref, x1_ref, h2_ref, *, d):
    ya = _dot(att_ref[...], wa_ref[...])
    ys = _dot(ssd_ref[...], ws_ref[...])
    gl = jax.nn.sigmoid(g_ref[...].astype(F32))
    mixed = gl[:, :d] * ya + gl[:, d:] * ys
    mo = _dot(mixed.astype(BF16), wo_ref[...])
    x1 = _layer_norm(ALPHA * x_ref[...] + gate_ref[0] * mo, lg_ref[...], lb_ref[...])
    x1_ref[...] = x1
    h2_ref[...] = x1 * (1.0 + sc2_ref[0]) + sh2_ref[0]


def merge_ln1(att2d, ssd2d, proj2d, x2d, gate1, shift2, scale2, w_a, w_s, w_o, ln_g, ln_b,
              g_col0, row0):
    n, d = att2d.shape[0], x2d.shape[1]
    tm = 512
    blk0 = row0 // tm
    cst3 = lambda i: (0, 0, 0)
    cst = lambda i: (0, 0)
    kern = functools.partial(_merge_kernel, d=d)
    return pl.pallas_call(
        kern,
        grid=(n // tm,),
        in_specs=[pl.BlockSpec((tm, att2d.shape[1]), lambda i: (i, 0)),
                  pl.BlockSpec((tm, ssd2d.shape[1]), lambda i: (i + blk0, 0)),
                  pl.BlockSpec((tm, 2 * d), lambda i: (i + blk0, g_col0 // (2 * d))),
                  pl.BlockSpec((tm, d), lambda i: (i + blk0, 0)),
                  pl.BlockSpec((1, 1, d), cst3),
                  pl.BlockSpec((1, 1, d), cst3),
                  pl.BlockSpec((1, 1, d), cst3),
                  pl.BlockSpec(w_a.shape, cst),
                  pl.BlockSpec(w_s.shape, cst),
                  pl.BlockSpec(w_o.shape, cst),
                  pl.BlockSpec((1, d), cst),
                  pl.BlockSpec((1, d), cst)],
        out_specs=[pl.BlockSpec((tm, d), lambda i: (i, 0)),
                   pl.BlockSpec((tm, d), lambda i: (i, 0))],
        out_shape=[jax.ShapeDtypeStruct((n, d), F32),
                   jax.ShapeDtypeStruct((n, d), F32)],
        compiler_params=_cparams(("parallel",)),
        name="merge_ln1",
    )(att2d, ssd2d, proj2d, x2d, gate1, shift2, scale2, w_a, w_s, w_o, ln_g, ln_b)


ROUTE_T = 1024


def _topk_rounds(scs, ids, k, payloads=None):
    n_p = len(scs)
    vals = [[] for _ in range(n_p)]
    sels = [[] for _ in range(n_p)]
    picks = [[] for _ in range(n_p)]
    scs = list(scs)
    for _ in range(k):
        ms = [jnp.max(sc, axis=0, keepdims=True) for sc in scs]
        ss = [jnp.min(jnp.where(scs[p] == ms[p], ids, float(1 << 24)), axis=0, keepdims=True)
              for p in range(n_p)]
        hits = [ids == ss[p] for p in range(n_p)]
        if payloads is not None:
            for p in range(n_p):
                picks[p].append(jnp.max(jnp.where(hits[p], payloads[p], -1.0), axis=0, keepdims=True))
        scs = [jnp.where(hits[p], -jnp.inf, scs[p]) for p in range(n_p)]
        for p in range(n_p):
            vals[p].append(ms[p])
            sels[p].append(ss[p])
    cat = lambda xs: jnp.concatenate(xs, axis=0)
    return [(cat(vals[p]), cat(sels[p]), cat(picks[p]) if payloads is not None else None)
            for p in range(n_p)]


STAIR_COUNT = (16, 8, 5, 4, 3, 2, 2, 2)
STAIR_ROWS = 16 + 8 * 7 + 8


def _stair_tables(t):
    K = PEER_TOPK
    r = lax.broadcasted_iota(jnp.int32, (STAIR_ROWS, t), 0)
    mid_a = 1 + ((r - K) >> 3)
    mid_b = (r - K) & 7
    a = jnp.where(r < K, 0, jnp.where(r < K + 56, mid_a, r - (K + 56) + 8))
    b = jnp.where(r < K, r, jnp.where(r < K + 56, mid_b, 0))
    count = jnp.full((STAIR_ROWS, t), 1, jnp.int32)
    for av, nb in enumerate(STAIR_COUNT):
        count = jnp.where(a == av, nb, count)
    valid = b < count
    pos_ids = (a * K + b).astype(F32)
    return valid, pos_ids


def _stair_candidates(v0, v1, i0, i1, valid):
    K = PEER_TOPK
    s_blocks = [v0[0:1, :] + v1]
    c_blocks = [i0[0:1, :] * float(PEER_N_KEYS) + i1]
    for a in range(1, 8):
        s_blocks.append(v0[a:a + 1, :] + v1[0:8, :])
        c_blocks.append(i0[a:a + 1, :] * float(PEER_N_KEYS) + i1[0:8, :])
    s_blocks.append(v0[8:K, :] + v1[0:1, :])
    c_blocks.append(i0[8:K, :] * float(PEER_N_KEYS) + i1[0:1, :])
    cand_s = jnp.where(valid, jnp.concatenate(s_blocks, axis=0), -jnp.inf)
    cand_i = jnp.concatenate(c_blocks, axis=0)
    return cand_s, cand_i


def _route_kernel(h2_ref, wq_ref, keys_ref, eidx_ref, gate_ref, q_ref, val_ref, idx_ref,
                  es_ref, gs_ref):
    t = ROUTE_T
    K = PEER_TOPK
    hb = h2_ref[...].astype(BF16)
    q = _dot(hb, wq_ref[...]).astype(BF16)
    for hj in range(2 * PEER_HEADS):
        q_ref[hj] = q[:, hj * PEER_HALF_DIM:(hj + 1) * PEER_HALF_DIM]

    key_ids = lax.broadcasted_iota(jnp.int32, (PEER_N_KEYS, t), 0).astype(F32)

    def level1(h, carry):
        scs = [_dot_nt(keys_ref[j], q_ref[2 * h + j]) for j in range(2)]
        res = _topk_rounds(scs, key_ids, K)
        for j in range(2):
            val_ref[2 * h + j] = res[j][0]
            idx_ref[2 * h + j] = res[j][1]
        return carry

    lax.fori_loop(0, PEER_HEADS, level1, 0)

    valid, pos_ids = _stair_tables(t)

    def level2(hp, carry):
        heads = [2 * hp, 2 * hp + 1]
        cands = [_stair_candidates(val_ref[2 * h], val_ref[2 * h + 1], idx_ref[2 * h], idx_ref[2 * h + 1], valid)
                 for h in heads]
        res = _topk_rounds([c[0] for c in cands], pos_ids, K, payloads=[c[1] for c in cands])
        for p, h in enumerate(heads):
            top_s = res[p][0]
            e = jnp.exp(top_s - top_s[0:1, :])
            gs_ref[h] = e / jnp.sum(e, axis=0, keepdims=True)
            es_ref[h] = res[p][2]
        return carry

    lax.fori_loop(0, PEER_HEADS // 2, level2, 0)

    eidx_ref[...] = es_ref[...].reshape(PEER_HEADS * K, t).T.astype(jnp.int32)
    gate_ref[...] = gs_ref[...].reshape(PEER_HEADS * K, t).T


def peer_route(h2, w_query, sub_keys):
    n, d = h2.shape
    t = ROUTE_T
    K = PEER_TOPK
    nsel = PEER_HEADS * K
    cst2 = lambda i: (0, 0)
    return pl.pallas_call(
        _route_kernel,
        grid=(n // t,),
        in_specs=[pl.BlockSpec((t, d), lambda i: (i, 0)),
                  pl.BlockSpec(w_query.shape, cst2),
                  pl.BlockSpec(sub_keys.shape, lambda i: (0, 0, 0))],
        out_specs=[pl.BlockSpec((t, nsel), lambda i: (i, 0)),
                   pl.BlockSpec((t, nsel), lambda i: (i, 0))],
        out_shape=[jax.ShapeDtypeStruct((n, nsel), jnp.int32),
                   jax.ShapeDtypeStruct((n, nsel), F32)],
        scratch_shapes=[pltpu.VMEM((2 * PEER_HEADS, t, PEER_HALF_DIM), BF16),
                        pltpu.VMEM((2 * PEER_HEADS, K, t), F32),
                        pltpu.VMEM((2 * PEER_HEADS, K, t), F32),
                        pltpu.VMEM((PEER_HEADS, K, t), F32),
                        pltpu.VMEM((PEER_HEADS, K, t), F32)],
        compiler_params=_cparams(("parallel",)),
        name="peer_route",
    )(h2, w_query, sub_keys)


SC_CORES = 2
SC_SUBCORES = 16
SC_WORKERS = SC_CORES * SC_SUBCORES
SC_LANES = 16
GATHER_ROWS = 32
GATHER_BUFS = 4
TOKEN_BLOCK = 8
COMBINE_COLS = 2
DOT_ROWS = 16


def pack_table(tab):
    e, d = tab.shape
    bits = lax.bitcast_convert_type(tab, jnp.uint32).reshape(e, d // (2 * SC_LANES), 2, SC_LANES)
    lo_src = lax.bitcast_convert_type(bits[:, :, 0, :], F32)
    low = lax.bitcast_convert_type(lo_src.astype(BF16), jnp.uint16).astype(jnp.int32)
    hi_bits = bits[:, :, 1, :]
    sign = hi_bits & jnp.uint32(0x80000000)
    mag = (hi_bits & jnp.uint32(0x7FFFFFFF)).astype(jnp.int32)
    high = jnp.maximum(mag + 0x8000 - low, 0) >> 16
    word = sign | (high.astype(jnp.uint32) << 16) | low.astype(jnp.uint32)
    return lax.bitcast_convert_type(word, jnp.int32).reshape(e, d // 2)


def _unpack_pair(w):
    lo = lax.bitcast_convert_type(jnp.left_shift(w, 16), F32)
    hi = lax.bitcast_convert_type(w, F32)
    return lo, hi


def _sc_mesh():
    return plsc.VectorSubcoreMesh(core_axis_name="c", subcore_axis_name="s")


def _worker_id():
    return lax.axis_index("s") * SC_CORES + lax.axis_index("c")


def _gather_pipeline(tab_hbm, idx_blk, rows_v, sems, n_chunks, chunks_per_tok, compute):
    ahead = GATHER_BUFS - 1

    def gather(q, buf):
        tok = q // chunks_per_tok
        ch = q % chunks_per_tok
        return pltpu.make_async_copy(
            tab_hbm.at[idx_blk.at[tok, pl.ds(ch * GATHER_ROWS, GATHER_ROWS)]],
            rows_v.at[buf], sems.at[buf])

    for q in range(ahead):
        gather(q, q).start()

    def ring(i, carry):
        for u in range(GATHER_BUFS):
            q = GATHER_BUFS * i + u

            @pl.when(q + ahead < n_chunks)
            def _():
                gather(q + ahead, (u + ahead) % GATHER_BUFS).start()

            gather(q, u).wait()
            compute(q, u)
        return carry

    lax.fori_loop(0, n_chunks // GATHER_BUFS, ring, 0)


def _block_pipeline(n_blocks, in_copies, out_copy, work):
    for cp in in_copies(0, 0):
        cp.start()

    def block(bi, carry):
        slot = bi % 2
        for cp in in_copies(bi, slot):
            cp.wait()

        @pl.when(bi + 1 < n_blocks)
        def _():
            for cp in in_copies(bi + 1, 1 - slot):
                cp.start()

        @pl.when(bi >= 2)
        def _():
            out_copy(bi - 2, slot).wait()

        work(bi, slot)
        out_copy(bi, slot).start()
        return carry

    lax.fori_loop(0, n_blocks, block, 0)
    out_copy(n_blocks - 2, n_blocks % 2).wait()
    out_copy(n_blocks - 1, (n_blocks - 1) % 2).wait()


def _dots_phase(u_hbm, h2_hbm, eidx_hbm, out_hbm, idx_v, x_v, rows_v, o_v, sems, in_sems, out_sems,
                n, d, nsel):
    tok_per_w = n // SC_WORKERS
    chunks_per_tok = nsel // GATHER_ROWS
    n_chunks = TOKEN_BLOCK * chunks_per_tok
    base = _worker_id() * tok_per_w

    def in_copies(bi, slot):
        t0 = base + bi * TOKEN_BLOCK
        return [pltpu.make_async_copy(eidx_hbm.at[pl.ds(t0, TOKEN_BLOCK)], idx_v.at[slot], in_sems.at[slot]),
                pltpu.make_async_copy(h2_hbm.at[pl.ds(t0, TOKEN_BLOCK)], x_v.at[slot], in_sems.at[slot])]

    def out_copy(bi, slot):
        t0 = base + bi * TOKEN_BLOCK
        return pltpu.make_async_copy(o_v.at[slot], out_hbm.at[pl.ds(t0, TOKEN_BLOCK)], out_sems.at[slot])

    def work(bi, slot):
        def compute(q, buf):
            tok = q // chunks_per_tok
            ch = q % chunks_per_tok
            zero = jnp.zeros((SC_LANES,), F32)
            for r0 in range(0, GATHER_ROWS, DOT_ROWS):
                def cbody(c, accs, r0=r0):
                    x_lo = x_v[slot, tok, pl.ds(c * 2 * SC_LANES, SC_LANES)]
                    x_hi = x_v[slot, tok, pl.ds(c * 2 * SC_LANES + SC_LANES, SC_LANES)]
                    out = []
                    for r in range(DOT_ROWS):
                        lo, hi = _unpack_pair(rows_v[buf, r0 + r, pl.ds(c * SC_LANES, SC_LANES)])
                        out.append(accs[r] + lo * x_lo + hi * x_hi)
                    return tuple(out)

                accs = lax.fori_loop(0, d // (2 * SC_LANES), cbody, tuple(zero for _ in range(DOT_ROWS)))
                for r in range(DOT_ROWS):
                    o_v[slot, tok, pl.ds((ch * GATHER_ROWS + r0 + r) * SC_LANES, SC_LANES)] = accs[r]

        _gather_pipeline(u_hbm, idx_v.at[slot], rows_v, sems, n_chunks, chunks_per_tok, compute)

    _block_pipeline(tok_per_w // TOKEN_BLOCK, in_copies, out_copy, work)


def _combine_phase(v_hbm, w_hbm, eidx_hbm, out_hbm, idx_v, w_v, rows_v, y_v, sems, in_sems, out_sems,
                   n, d, nsel):
    tok_per_w = n // SC_WORKERS
    chunks_per_tok = nsel // GATHER_ROWS
    n_chunks = TOKEN_BLOCK * chunks_per_tok
    base = _worker_id() * tok_per_w

    def in_copies(bi, slot):
        t0 = base + bi * TOKEN_BLOCK
        return [pltpu.make_async_copy(eidx_hbm.at[pl.ds(t0, TOKEN_BLOCK)], idx_v.at[slot], in_sems.at[slot]),
                pltpu.make_async_copy(w_hbm.at[pl.ds(t0, TOKEN_BLOCK)], w_v.at[slot], in_sems.at[slot])]

    def out_copy(bi, slot):
        t0 = base + bi * TOKEN_BLOCK
        return pltpu.make_async_copy(y_v.at[slot], out_hbm.at[pl.ds(t0, TOKEN_BLOCK)], out_sems.at[slot])

    def work(bi, slot):
        slot_vec = jnp.full((SC_LANES,), slot, jnp.int32)

        def compute(q, buf):
            tok = q // chunks_per_tok
            ch = q % chunks_per_tok
            tok_vec = jnp.full((SC_LANES,), tok, jnp.int32)
            ws = [plsc.load_gather(
                w_v, [slot_vec, tok_vec, jnp.full((SC_LANES,), ch * GATHER_ROWS + r, jnp.int32)])
                for r in range(GATHER_ROWS)]
            first = ch == 0

            def cbody(cg, carry):
                cols = [cg * COMBINE_COLS + u for u in range(COMBINE_COLS)]
                sl_lo = [pl.ds(c * 2 * SC_LANES, SC_LANES) for c in cols]
                sl_hi = [pl.ds(c * 2 * SC_LANES + SC_LANES, SC_LANES) for c in cols]
                zero = jnp.zeros((SC_LANES,), F32)
                a_lo = [jnp.where(first, zero, y_v[slot, tok, s]) for s in sl_lo]
                a_hi = [jnp.where(first, zero, y_v[slot, tok, s]) for s in sl_hi]
                for r in range(GATHER_ROWS):
                    for u in range(COMBINE_COLS):
                        lo, hi = _unpack_pair(rows_v[buf, r, pl.ds(cols[u] * SC_LANES, SC_LANES)])
                        a_lo[u] = a_lo[u] + ws[r] * lo
                        a_hi[u] = a_hi[u] + ws[r] * hi
                for u in range(COMBINE_COLS):
                    y_v[slot, tok, sl_lo[u]] = a_lo[u]
                    y_v[slot, tok, sl_hi[u]] = a_hi[u]
                return carry

            lax.fori_loop(0, d // (2 * SC_LANES) // COMBINE_COLS, cbody, 0)

        _gather_pipeline(v_hbm, idx_v.at[slot], rows_v, sems, n_chunks, chunks_per_tok, compute)

    _block_pipeline(tok_per_w // TOKEN_BLOCK, in_copies, out_copy, work)


def peer_experts_sc(u_pk=None, h2=None, eidx=None, v_pk=None, wts_prev=None, eidx_prev=None):
    do_dots = u_pk is not None
    do_comb = v_pk is not None
    n_c = wts_prev.shape[0] if do_comb else 0
    n_d = h2.shape[0] if do_dots else 0
    nsel = (eidx if do_dots else eidx_prev).shape[1]
    d = 2 * (u_pk if do_dots else v_pk).shape[1]
    out_types, scratch, args = [], [], []
    if do_comb:
        out_types.append(jax.ShapeDtypeStruct((n_c, d), F32))
        args += [v_pk, wts_prev, eidx_prev]
        scratch += [pltpu.VMEM((2, TOKEN_BLOCK, nsel), F32)]
    if do_dots:
        out_types.append(jax.ShapeDtypeStruct((n_d, nsel * SC_LANES), F32))
        args += [u_pk, h2, eidx]
        scratch += [pltpu.VMEM((2, TOKEN_BLOCK, nsel * SC_LANES), F32)]
    scratch += [pltpu.VMEM((2, TOKEN_BLOCK, d), F32),
                pltpu.VMEM((2, TOKEN_BLOCK, nsel), jnp.int32),
                pltpu.VMEM((GATHER_BUFS, GATHER_ROWS, d // 2), jnp.int32),
                pltpu.SemaphoreType.DMA((GATHER_BUFS,)),
                pltpu.SemaphoreType.DMA((2,)),
                pltpu.SemaphoreType.DMA((2,))]

    def body(*refs):
        refs = list(refs)
        ins = refs[:len(args)]
        outs = refs[len(args):len(args) + len(out_types)]
        scr = refs[len(args) + len(out_types):]
        xy_v, idx_v, rows_v, sems, in_sems, out_sems = scr[-6:]
        if do_comb:
            v_hbm, w_hbm, ep_hbm = ins[:3]
            _combine_phase(v_hbm, w_hbm, ep_hbm, outs[0], idx_v, scr[0], rows_v, xy_v,
                           sems, in_sems, out_sems, n_c, d, nsel)
        if do_dots:
            u_hbm, h2_hbm, e_hbm = ins[-3:]
            _dots_phase(u_hbm, h2_hbm, e_hbm, outs[-1], idx_v, xy_v, rows_v, scr[-7],
                        sems, in_sems, out_sems, n_d, d, nsel)

    res = pl.kernel(
        body, out_type=tuple(out_types), mesh=_sc_mesh(), scratch_types=scratch,
        compiler_params=pltpu.CompilerParams(needs_layout_passes=False),
        name="peer_experts_sc" + ("_c" if do_comb else "") + ("_d" if do_dots else ""),
    )(*args)
    res = list(res)
    y_prev = res[0] if do_comb else None
    part = res[-1] if do_dots else None
    return y_prev, part


def _weights_kernel(part_ref, gate_ref, red_ref, o_ref):
    act = jnp.dot(part_ref[...], red_ref[...], preferred_element_type=F32,
                  precision=lax.Precision.HIGHEST)
    gelu = 0.5 * act * (1.0 + lax.erf(act * (2.0 ** -0.5)))
    o_ref[...] = gate_ref[...] * gelu


def peer_weights(part2d, gate):
    n, nsel = gate.shape
    tm = 512
    w = part2d.shape[1]
    red = (jnp.arange(w)[:, None] // SC_LANES == jnp.arange(nsel)[None, :]).astype(F32)
    return pl.pallas_call(
        _weights_kernel,
        grid=(n // tm,),
        in_specs=[pl.BlockSpec((tm, w), lambda i: (i, 0)),
                  pl.BlockSpec((tm, nsel), lambda i: (i, 0)),
                  pl.BlockSpec((w, nsel), lambda i: (0, 0))],
        out_specs=pl.BlockSpec((tm, nsel), lambda i: (i, 0)),
        out_shape=jax.ShapeDtypeStruct((n, nsel), F32),
        compiler_params=_cparams(("parallel",)),
        name="peer_weights",
    )(part2d, gate, red)


def _final_kernel(x1_ref, y_ref, gate_ref, lg_ref, lb_ref, o_ref):
    o_ref[...] = _layer_norm(ALPHA * x1_ref[...] + gate_ref[0] * y_ref[...], lg_ref[...], lb_ref[...])


def final_ln(x1, y_ffn, gate2, ln_g, ln_b):
    n, d = x1.shape
    tm = 1024
    return pl.pallas_call(
        _final_kernel,
        grid=(n // tm,),
        in_specs=[pl.BlockSpec((tm, d), lambda i: (i, 0)),
                  pl.BlockSpec((tm, d), lambda i: (i, 0)),
                  pl.BlockSpec((1, 1, d), lambda i: (0, 0, 0)),
                  pl.BlockSpec((1, d), lambda i: (0, 0)),
                  pl.BlockSpec((1, d), lambda i: (0, 0))],
        out_specs=pl.BlockSpec((tm, d), lambda i: (i, 0)),
        out_shape=jax.ShapeDtypeStruct((n, d), F32),
        compiler_params=_cparams(("parallel",)),
        name="final_ln",
    )(x1, y_ffn, gate2, ln_g, ln_b)


def _cumsum(widths):
    out, t = [], 0
    for w in widths:
        t += w
        out.append(t)
    return out


def _block(x, c, w_ada, b_ada, w_in, conv_w, conv_b, dt_bias, a_log, d_skip, ssd_norm_w,
           lambda_q1, lambda_k1, lambda_q2, lambda_k2, da_subln_w, w_attn_branch, w_ssd_branch,
           w_out, ln1_g, ln1_b, peer_w_query, peer_sub_keys, peer_u, peer_v, ln2_g, ln2_b,
           layer):
    bsz, s, d = x.shape
    qk_w = DA_HEADS * 2 * DA_HEAD_DIM
    v_w = DA_HEADS * DA_V_DIM
    d_inner = w_ssd_branch.shape[0]
    n_heads = d_inner // SSD_HEAD_DIM
    bc_w = SSD_GROUPS * SSD_D_STATE
    xbc_w = d_inner + 2 * bc_w

    c_pad = jnp.pad(c, ((0, 8 - bsz), (0, 0)))
    mod = ada_mod(c_pad, w_ada, b_ada)[:bsz]
    shift1, scale1, gate1, shift2, scale2, gate2 = [m.reshape(bsz, 1, d) for m in jnp.split(mod, 6, axis=-1)]

    splits = _cumsum([qk_w, qk_w, v_w, d_inner, xbc_w, n_heads, 2 * d])
    wq, wk, wv, wz, wxbc, wdt, wg = [w_in[:, a:b] for a, b in zip([0] + splits[:-1], splits)]
    wq = wq * ((DA_HEAD_DIM ** -0.5) * LOG2E)
    w_main = jnp.concatenate([wxbc, wq, wz, wg, wk, wv], axis=1).astype(BF16)
    w_dt = jnp.pad(wdt, ((0, 0), (0, 128 - n_heads))).astype(BF16)
    xbc_c0 = 0
    q_c0 = xbc_w
    z_c0 = q_c0 + qk_w
    g_c0 = z_c0 + d_inner
    k_c0 = g_c0 + 2 * d
    v_c0 = k_c0 + qk_w
    lambda_init = 0.8 - 0.6 * math.exp(-0.3 * layer)
    lam_params = jnp.stack([lambda_q1, lambda_k1, lambda_q2, lambda_k2]).astype(F32)
    pad_h = lambda t: jnp.pad(t.astype(F32), (0, 128 - n_heads)).reshape(1, 128)
    dtb_p, alog_p = pad_h(dt_bias), pad_h(a_log)
    dskip_e = jnp.repeat(d_skip.astype(F32), SSD_HEAD_DIM).reshape(1, d_inner)
    w_ab, w_sb, w_o = w_attn_branch.astype(BF16), w_ssd_branch.astype(BF16), w_out.astype(BF16)
    w_qp, keys_b = peer_w_query.astype(BF16), peer_sub_keys.astype(BF16)
    u_pk, v_pk = pack_table(peer_u), pack_table(peer_v)

    outs = []
    pending = None

    def retire(cur):
        nonlocal pending
        kw = {}
        if pending is not None:
            p_x1, p_gate2, p_eidx, p_gate, p_part = pending
            wts = peer_weights(p_part, p_gate)
            kw.update(v_pk=v_pk, wts_prev=wts, eidx_prev=p_eidx)
        if cur is not None:
            kw.update(u_pk=u_pk, h2=cur[4], eidx=cur[2])
        y_prev, part = peer_experts_sc(**kw)
        if pending is not None:
            outs.append(final_ln(p_x1, y_prev, p_gate2, ln2_g.reshape(1, d), ln2_b.reshape(1, d)))
        pending = None if cur is None else (cur[0], cur[1], cur[2], cur[3], part)

    for b in range(bsz):
        sl = slice(b, b + 1)
        xb = x[b]
        proj, dt_raw = in_proj(xb, shift1[sl], scale1[sl], w_main, w_dt, s)
        proj3 = proj.reshape(1, s, proj.shape[1])
        y_ssd = ssd_branch(proj3, dt_raw.reshape(1, s, 128), conv_w, conv_b.reshape(1, xbc_w),
                           dtb_p, alog_p, dskip_e, ssd_norm_w.reshape(1, d_inner),
                           xbc_c0, z_c0, d_inner, n_heads).reshape(s, d_inner)
        row0 = 0
        for gs in _group_sizes(s, first=(b == 0), last=(b == bsz - 1)):
            y_attn = diff_attention(proj3, lam_params, da_subln_w.reshape(1, DA_V_DIM), q_c0, k_c0, v_c0,
                                    lambda_init, row0 // ATT_TQ, gs // ATT_TQ)
            x1, h2 = merge_ln1(y_attn.reshape(gs, v_w), y_ssd, proj, xb,
                               gate1[sl], shift2[sl], scale2[sl], w_ab, w_sb, w_o,
                               ln1_g.reshape(1, d), ln1_b.reshape(1, d), g_c0, row0)
            eidx, gate = peer_route(h2, w_qp, keys_b)
            retire((x1, gate2[sl], eidx, gate, h2))
            row0 += gs
    retire(None)
    return jnp.concatenate(outs, axis=0).reshape(bsz, s, d)


def _group_sizes(s, first, last):
    e = s // 8
    if first and last:
        return [e, 3 * e, 2 * e, e, e]
    if first:
        return [e, 3 * e, 4 * e]
    if last:
        return [4 * e, 2 * e, e, e]
    return [4 * e, 4 * e]


def kernel(x, c, w_ada, b_ada, w_in, conv_w, conv_b, dt_bias, a_log, d_skip, ssd_norm_w,
           lambda_q1, lambda_k1, lambda_q2, lambda_k2, da_subln_w, w_attn_branch, w_ssd_branch,
           w_out, ln1_g, ln1_b, peer_w_query, peer_sub_keys, peer_u, peer_v, ln2_g, ln2_b):
    for l in range(w_ada.shape[0]):
        x = _block(x, c, w_ada[l], b_ada[l], w_in[l], conv_w[l], conv_b[l], dt_bias[l], a_log[l],
                   d_skip[l], ssd_norm_w[l], lambda_q1[l], lambda_k1[l], lambda_q2[l], lambda_k2[l],
                   da_subln_w[l], w_attn_branch[l], w_ssd_branch[l], w_out[l], ln1_g[l], ln1_b[l],
                   peer_w_query[l], peer_sub_keys[l], peer_u[l], peer_v[l], ln2_g[l], ln2_b[l], l)
    return x
```

```python
import functools
import math

import jax
import jax.numpy as jnp
from jax import lax
from jax.experimental import pallas as pl
from jax.experimental.pallas import tpu as pltpu
from jax.experimental.pallas import tpu_sc as plsc

F32 = jnp.float32
BF16 = jnp.bfloat16

DA_HEADS = 8
DA_HEAD_DIM = 64
DA_V_DIM = 2 * DA_HEAD_DIM
SSD_HEAD_DIM = 64
SSD_GROUPS = 4
SSD_D_STATE = 128
SSD_CONV = 4
SSD_CHUNK = 512
PEER_N_KEYS = 128
PEER_HEADS = 8
PEER_TOPK = 16
PEER_HALF_DIM = 128
DEPTH = 1
ALPHA = (2 * DEPTH) ** 0.25
EPS = 1e-5
LOG2E = 1.4426950408889634
NEG_BIG = -1e30

VMEM_LIMIT_BYTES = 56 * 1024 * 1024


def _cparams(sem):
    return pltpu.CompilerParams(dimension_semantics=sem, vmem_limit_bytes=VMEM_LIMIT_BYTES)


def _dot(a, b):
    return jnp.dot(a, b, preferred_element_type=F32)


def _dot_nt(a, b):
    return lax.dot_general(a, b, (((1,), (1,)), ((), ())), preferred_element_type=F32)


def _ada_kernel(c_ref, w_ref, b_ref, o_ref):
    c = c_ref[...]
    sc = c * jax.nn.sigmoid(c)
    o_ref[...] = jnp.dot(sc, w_ref[...], preferred_element_type=F32,
                         precision=lax.Precision.HIGHEST) + b_ref[...]


def ada_mod(c_pad, w_ada, b_ada):
    d, n = w_ada.shape
    tn = 512
    return pl.pallas_call(
        _ada_kernel,
        grid=(n // tn,),
        in_specs=[pl.BlockSpec((c_pad.shape[0], d), lambda j: (0, 0)),
                  pl.BlockSpec((d, tn), lambda j: (0, j)),
                  pl.BlockSpec((1, tn), lambda j: (0, j))],
        out_specs=pl.BlockSpec((c_pad.shape[0], tn), lambda j: (0, j)),
        out_shape=jax.ShapeDtypeStruct((c_pad.shape[0], n), F32),
        compiler_params=_cparams(("arbitrary",)),
        name="ada_mod",
    )(c_pad, w_ada, b_ada.reshape(1, n))


def _inproj_kernel(x_ref, sh_ref, sc_ref, w_ref, wdt_ref, o_ref, dt_ref, h_ref):
    @pl.when(pl.program_id(1) == 0)
    def _():
        h = x_ref[...] * (1.0 + sc_ref[0]) + sh_ref[0]
        hb = h.astype(BF16)
        h_ref[...] = hb
        dt_ref[...] = _dot(hb, wdt_ref[...])

    o_ref[...] = _dot(h_ref[...], w_ref[...]).astype(BF16)


def in_proj(x2d, shift1, scale1, w_main, w_dt, seq):
    n, d = x2d.shape
    tm, tn = 2048, 1024
    tiles_per_batch = seq // tm
    width = w_main.shape[1]
    bmap = lambda i, j: (i // tiles_per_batch, 0, 0)
    return pl.pallas_call(
        _inproj_kernel,
        grid=(n // tm, width // tn),
        in_specs=[pl.BlockSpec((tm, d), lambda i, j: (i, 0)),
                  pl.BlockSpec((1, 1, d), bmap),
                  pl.BlockSpec((1, 1, d), bmap),
                  pl.BlockSpec((d, tn), lambda i, j: (0, j)),
                  pl.BlockSpec((d, 128), lambda i, j: (0, 0))],
        out_specs=[pl.BlockSpec((tm, tn), lambda i, j: (i, j)),
                   pl.BlockSpec((tm, 128), lambda i, j: (i, 0))],
        out_shape=[jax.ShapeDtypeStruct((n, width), BF16),
                   jax.ShapeDtypeStruct((n, 128), F32)],
        scratch_shapes=[pltpu.VMEM((tm, d), BF16)],
        compiler_params=_cparams(("parallel", "arbitrary")),
        name="in_proj",
    )(x2d, shift1, scale1, w_main, w_dt)


ATT_TQ = 256
ATT_TK = 512


def _attn_kernel(q_ref, k_ref, v_ref, lam_ref, w_ref, o_ref,
                 vt_ref, sa_ref, sb_ref, bias_ref, m_ref, l_ref, acc_ref, *, kv_len, q_block0, lambda_init):
    tq, tk = ATT_TQ, ATT_TK
    h = pl.program_id(1)
    qi = pl.program_id(2) + q_block0
    n_kb_total = kv_len // tk

    @pl.when(pl.program_id(2) == 0)
    def _():
        for c in range(n_kb_total):
            blk = v_ref[0, c * tk:(c + 1) * tk, :].astype(F32)
            vt_ref[c] = blk.T.astype(BF16)

    slope2 = jnp.exp2(-(h + 1).astype(F32)) * LOG2E
    i0 = qi * tq
    kb_diag = i0 // tk
    q = q_ref[0]
    qs = [q[:, c * DA_HEAD_DIM:(c + 1) * DA_HEAD_DIM] for c in range(2)]
    rows = lax.broadcasted_iota(jnp.int32, (tk, tq), 0)
    cols = lax.broadcasted_iota(jnp.int32, (tk, tq), 1)
    bias0 = rows.astype(F32) * slope2
    bias_ref[0] = bias0
    bias_ref[1] = jnp.where(rows + kb_diag * tk <= cols + i0, bias0, NEG_BIG)

    m_ref[...] = jnp.full(m_ref.shape, NEG_BIG, F32)
    l_ref[...] = jnp.zeros(l_ref.shape, F32)
    acc_ref[...] = jnp.zeros(acc_ref.shape, F32)

    def scores(kb, s_ref):
        kblk = k_ref[0, pl.ds(pl.multiple_of(kb * tk, tk), tk), :]
        for c in range(2):
            s_ref[c] = _dot_nt(kblk[:, c * DA_HEAD_DIM:(c + 1) * DA_HEAD_DIM], qs[c])

    def softmax_pv(kb, s_ref):
        vt = vt_ref[kb]
        cb = slope2 * (kb * tk - i0).astype(F32)
        bias = bias_ref[(kb == kb_diag).astype(jnp.int32)]
        for c in range(2):
            s = s_ref[c] + bias
            m_old = m_ref[c]
            m_new = jnp.maximum(m_old, jnp.max(s, axis=0, keepdims=True) + cb)
            p = jnp.exp2(s - (m_new - cb))
            alpha = jnp.exp2(m_old - m_new)
            l_ref[c] = alpha * l_ref[c] + jnp.sum(p, axis=0, keepdims=True)
            acc_ref[c] = alpha * acc_ref[c] + _dot(vt, p.astype(BF16))
            m_ref[c] = m_new

    n_blocks = kb_diag + 1
    scores(0, sa_ref)

    def body(i, carry):
        kb = 2 * i
        scores(kb + 1, sb_ref)
        softmax_pv(kb, sa_ref)
        scores(jnp.minimum(kb + 2, kb_diag), sa_ref)
        softmax_pv(kb + 1, sb_ref)
        return carry

    lax.fori_loop(0, n_blocks // 2, body, 0)

    @pl.when(n_blocks % 2 == 1)
    def _():
        softmax_pv(kb_diag, sa_ref)

    lam_p = lam_ref[...]
    lam = (jnp.exp(jnp.sum(lam_p[0:1] * lam_p[1:2])) - jnp.exp(jnp.sum(lam_p[2:3] * lam_p[3:4]))
           + lambda_init)
    o_t = acc_ref[0] / l_ref[0] - lam * (acc_ref[1] / l_ref[1])
    o = o_t.T
    o = o * lax.rsqrt(jnp.mean(o * o, axis=-1, keepdims=True) + EPS) * w_ref[...]
    o_ref[0] = (o * (1.0 - lambda_init)).astype(BF16)


def diff_attention(proj3, lam_params, subln_w, q_col0, k_col0, v_col0, lambda_init, q_block0, n_q_blocks):
    b, s, _ = proj3.shape
    tq, tk = ATT_TQ, ATT_TK
    kv_len = -(-((q_block0 + n_q_blocks) * tq) // tk) * tk
    qb, kb, vb = q_col0 // 128, k_col0 // 128, v_col0 // 128
    kern = functools.partial(_attn_kernel, kv_len=kv_len, q_block0=q_block0, lambda_init=lambda_init)
    return pl.pallas_call(
        kern,
        grid=(b, DA_HEADS, n_q_blocks),
        in_specs=[pl.BlockSpec((1, tq, 128), lambda bi, h, qi: (bi, qi + q_block0, qb + h)),
                  pl.BlockSpec((1, kv_len, 128), lambda bi, h, qi: (bi, 0, kb + h)),
                  pl.BlockSpec((1, kv_len, 128), lambda bi, h, qi: (bi, 0, vb + h)),
                  pl.BlockSpec((4, DA_HEAD_DIM), lambda bi, h, qi: (0, 0)),
                  pl.BlockSpec((1, DA_V_DIM), lambda bi, h, qi: (0, 0))],
        out_specs=pl.BlockSpec((1, tq, 128), lambda bi, h, qi: (bi, qi, h)),
        out_shape=jax.ShapeDtypeStruct((b, n_q_blocks * tq, DA_HEADS * DA_V_DIM), BF16),
        scratch_shapes=[pltpu.VMEM((kv_len // tk, DA_V_DIM, tk), BF16),
                        pltpu.VMEM((2, tk, tq), F32),
                        pltpu.VMEM((2, tk, tq), F32),
                        pltpu.VMEM((2, tk, tq), F32),
                        pltpu.VMEM((2, 1, tq), F32),
                        pltpu.VMEM((2, 1, tq), F32),
                        pltpu.VMEM((2, DA_V_DIM, tq), F32)],
        compiler_params=_cparams(("parallel", "arbitrary", "arbitrary")),
        name="diff_attn",
    )(proj3, proj3, proj3, lam_params, subln_w)


def _split3(x):
    hi = x.astype(BF16)
    r1 = x - hi.astype(F32)
    mid = r1.astype(BF16)
    lo = (r1 - mid.astype(F32)).astype(BF16)
    return hi, mid, lo


def _ssd_kernel(xbc_ref, z_ref, dt_ref, cw_ref, cb_ref, dtb_ref, alog_ref, dsk_ref, nw_ref,
                o_ref, ext_ref, state_ref, y_ref, *, d_inner, n_heads):
    L = SSD_CHUNK
    P = SSD_HEAD_DIM
    NS = SSD_D_STATE
    G = SSD_GROUPS
    R = n_heads // G
    bc_w = G * NS
    ci = pl.program_id(1)

    @pl.when(ci == 0)
    def _():
        state_ref[...] = jnp.zeros(state_ref.shape, F32)
        ext_ref[0:8, :] = jnp.zeros((8, ext_ref.shape[1]), F32)

    ext_ref[8:8 + L, :] = xbc_ref[0].astype(F32)
    conv = cb_ref[...]
    for k in range(SSD_CONV):
        off = 8 - (SSD_CONV - 1) + k
        conv = conv + cw_ref[k:k + 1, :] * ext_ref[off:off + L, :]
    ext_ref[0:8, :] = ext_ref[L:L + 8, :]
    xc = conv * jax.nn.sigmoid(conv)

    dtr = dt_ref[0] + dtb_ref[...]
    dt = jnp.maximum(dtr, 0.0) + jnp.log1p(jnp.exp(-jnp.abs(dtr)))
    a = -jnp.exp(alog_ref[...])
    da = dt * a
    tri = (lax.broadcasted_iota(jnp.int32, (L, L), 0) >= lax.broadcasted_iota(jnp.int32, (L, L), 1))
    tri_b = tri.astype(BF16)
    hi, mid, lo = _split3(da)
    cum = _dot(tri_b, hi) + _dot(tri_b, mid) + _dot(tri_b, lo)
    cum_t = cum.T
    dt_t = dt.T
    cum_last = cum[L - 1:L, :]
    e_cum = jnp.exp(cum)
    dec_end = jnp.exp(cum_last - cum) * dt
    e_last = jnp.exp(cum_last)

    for g in range(G):
        bg = xc[:, d_inner + g * NS:d_inner + (g + 1) * NS]
        cg = xc[:, d_inner + bc_w + g * NS:d_inner + bc_w + (g + 1) * NS]
        bg_b = bg.astype(BF16)
        cg_b = cg.astype(BF16)
        bgt_b = bg.T.astype(BF16)
        cbm = _dot_nt(cg_b, bg_b)
        st_g = state_ref[g]
        y_state = _dot(cg_b, st_g.astype(BF16))
        for r in range(R):
            hh = g * R + r
            xh = xc[:, hh * P:(hh + 1) * P]
            seg = cum[:, hh:hh + 1] - cum_t[hh:hh + 1, :]
            w = jnp.where(tri, jnp.exp(seg), 0.0) * cbm * dt_t[hh:hh + 1, :]
            yh = _dot(w.astype(BF16), xh.astype(BF16))
            yh = yh + y_state[:, r * P:(r + 1) * P] * e_cum[:, hh:hh + 1]
            y_ref[:, hh * P:(hh + 1) * P] = yh
            dx = (xh * dec_end[:, hh:hh + 1]).astype(BF16)
            state_ref[g, :, r * P:(r + 1) * P] = (
                st_g[:, r * P:(r + 1) * P] * e_last[:, hh:hh + 1] + _dot(bgt_b, dx))

    xs = xc[:, :d_inner]
    z = z_ref[0].astype(F32)
    y = (y_ref[...] + dsk_ref[...] * xs) * (z * jax.nn.sigmoid(z))
    gw = d_inner // G
    for g in range(G):
        yg = y[:, g * gw:(g + 1) * gw]
        yn = yg * lax.rsqrt(jnp.mean(yg * yg, axis=-1, keepdims=True) + EPS)
        o_ref[0, :, g * gw:(g + 1) * gw] = (yn * nw_ref[:, g * gw:(g + 1) * gw]).astype(BF16)


def ssd_branch(proj3, dt3, conv_w, conv_b, dt_bias_p, a_log_p, dskip_e, norm_w,
               xbc_col0, z_col0, d_inner, n_heads):
    b, s, _ = proj3.shape
    L = SSD_CHUNK
    xw = conv_w.shape[1]
    kern = functools.partial(_ssd_kernel, d_inner=d_inner, n_heads=n_heads)
    cst = lambda bi, ci: (0, 0)
    return pl.pallas_call(
        kern,
        grid=(b, s // L),
        in_specs=[pl.BlockSpec((1, L, xw), lambda bi, ci: (bi, ci, xbc_col0 // xw)),
                  pl.BlockSpec((1, L, d_inner), lambda bi, ci: (bi, ci, z_col0 // d_inner)),
                  pl.BlockSpec((1, L, 128), lambda bi, ci: (bi, ci, 0)),
                  pl.BlockSpec((SSD_CONV, xw), cst),
                  pl.BlockSpec((1, xw), cst),
                  pl.BlockSpec((1, 128), cst),
                  pl.BlockSpec((1, 128), cst),
                  pl.BlockSpec((1, d_inner), cst),
                  pl.BlockSpec((1, d_inner), cst)],
        out_specs=pl.BlockSpec((1, L, d_inner), lambda bi, ci: (bi, ci, 0)),
        out_shape=jax.ShapeDtypeStruct((b, s, d_inner), BF16),
        scratch_shapes=[pltpu.VMEM((L + 8, xw), F32),
                        pltpu.VMEM((SSD_GROUPS, SSD_D_STATE, d_inner // SSD_GROUPS), F32),
                        pltpu.VMEM((L, d_inner), F32)],
        compiler_params=_cparams(("parallel", "arbitrary")),
        name="ssd_scan",
    )(proj3, proj3, dt3, conv_w, conv_b, dt_bias_p, a_log_p, dskip_e, norm_w)


def _layer_norm(v, g, b):
    mu = jnp.mean(v, axis=-1, keepdims=True)
    d = v - mu
    var = jnp.mean(d * d, axis=-1, keepdims=True)
    return d * lax.rsqrt(var + EPS) * g + b


def _merge_kernel(att_ref, ssd_ref, g_ref, x_ref, gate_ref, sh2_ref, sc2_ref,
                  wa_ref, ws_ref, wo_ref, lg_ref, lb_ref, x1_ref, h2_ref, *, d):
    ya = _dot(att_ref[...], wa_ref[...])
    ys = _dot(ssd_ref[...], ws_ref[...])
    gl = jax.nn.sigmoid(g_ref[...].astype(F32))
    mixed = gl[:, :d] * ya + gl[:, d:] * ys
    mo = _dot(mixed.astype(BF16), wo_ref[...])
    x1 = _layer_norm(ALPHA * x_ref[...] + gate_ref[0] * mo, lg_ref[...], lb_ref[...])
    x1_ref[...] = x1
    h2_ref[...] = x1 * (1.0 + sc2_ref[0]) + sh2_ref[0]


def merge_ln1(att2d, ssd2d, proj2d, x2d, gate1, shift2, scale2, w_a, w_s, w_o, ln_g, ln_b,
              g_col0, row0):
    n, d = att2d.shape[0], x2d.shape[1]
    tm = 512
    blk0 = row0 // tm
    cst3 = lambda i: (0, 0, 0)
    cst = lambda i: (0, 0)
    kern = functools.partial(_merge_kernel, d=d)
    return pl.pallas_call(
        kern,
        grid=(n // tm,),
        in_specs=[pl.BlockSpec((tm, att2d.shape[1]), lambda i: (i, 0)),
                  pl.BlockSpec((tm, ssd2d.shape[1]), lambda i: (i + blk0, 0)),
                  pl.BlockSpec((tm, 2 * d), lambda i: (i + blk0, g_col0 // (2 * d))),
                  pl.BlockSpec((tm, d), lambda i: (i + blk0, 0)),
                  pl.BlockSpec((1, 1, d), cst3),
                  pl.BlockSpec((1, 1, d), cst3),
                  pl.BlockSpec((1, 1, d), cst3),
                  pl.BlockSpec(w_a.shape, cst),
                  pl.BlockSpec(w_s.shape, cst),
                  pl.BlockSpec(w_o.shape, cst),
                  pl.BlockSpec((1, d), cst),
                  pl.BlockSpec((1, d), cst)],
        out_specs=[pl.BlockSpec((tm, d), lambda i: (i, 0)),
                   pl.BlockSpec((tm, d), lambda i: (i, 0))],
        out_shape=[jax.ShapeDtypeStruct((n, d), F32),
                   jax.ShapeDtypeStruct((n, d), F32)],
        compiler_params=_cparams(("parallel",)),
        name="merge_ln1",
    )(att2d, ssd2d, proj2d, x2d, gate1, shift2, scale2, w_a, w_s, w_o, ln_g, ln_b)


ROUTE_T = 1024


def _topk_rounds(scs, ids, k, payloads=None):
    n_p = len(scs)
    vals = [[] for _ in range(n_p)]
    sels = [[] for _ in range(n_p)]
    picks = [[] for _ in range(n_p)]
    scs = list(scs)
    for _ in range(k):
        ms = [jnp.max(sc, axis=0, keepdims=True) for sc in scs]
        ss = [jnp.min(jnp.where(scs[p] == ms[p], ids, float(1 << 24)), axis=0, keepdims=True)
              for p in range(n_p)]
        hits = [ids == ss[p] for p in range(n_p)]
        if payloads is not None:
            for p in range(n_p):
                picks[p].append(jnp.max(jnp.where(hits[p], payloads[p], -1.0), axis=0, keepdims=True))
        scs = [jnp.where(hits[p], -jnp.inf, scs[p]) for p in range(n_p)]
        for p in range(n_p):
            vals[p].append(ms[p])
            sels[p].append(ss[p])
    cat = lambda xs: jnp.concatenate(xs, axis=0)
    return [(cat(vals[p]), cat(sels[p]), cat(picks[p]) if payloads is not None else None)
            for p in range(n_p)]


STAIR_COUNT = (16, 8, 5, 4, 3, 2, 2, 2)
STAIR_ROWS = 16 + 8 * 7 + 8


def _stair_tables(t):
    K = PEER_TOPK
    r = lax.broadcasted_iota(jnp.int32, (STAIR_ROWS, t), 0)
    mid_a = 1 + ((r - K) >> 3)
    mid_b = (r - K) & 7
    a = jnp.where(r < K, 0, jnp.where(r < K + 56, mid_a, r - (K + 56) + 8))
    b = jnp.where(r < K, r, jnp.where(r < K + 56, mid_b, 0))
    count = jnp.full((STAIR_ROWS, t), 1, jnp.int32)
    for av, nb in enumerate(STAIR_COUNT):
        count = jnp.where(a == av, nb, count)
    valid = b < count
    pos_ids = (a * K + b).astype(F32)
    return valid, pos_ids


def _stair_candidates(v0, v1, i0, i1, valid):
    K = PEER_TOPK
    s_blocks = [v0[0:1, :] + v1]
    c_blocks = [i0[0:1, :] * float(PEER_N_KEYS) + i1]
    for a in range(1, 8):
        s_blocks.append(v0[a:a + 1, :] + v1[0:8, :])
        c_blocks.append(i0[a:a + 1, :] * float(PEER_N_KEYS) + i1[0:8, :])
    s_blocks.append(v0[8:K, :] + v1[0:1, :])
    c_blocks.append(i0[8:K, :] * float(PEER_N_KEYS) + i1[0:1, :])
    cand_s = jnp.where(valid, jnp.concatenate(s_blocks, axis=0), -jnp.inf)
    cand_i = jnp.concatenate(c_blocks, axis=0)
    return cand_s, cand_i


def _route_kernel(h2_ref, wq_ref, keys_ref, eidx_ref, gate_ref, q_ref, val_ref, idx_ref,
                  es_ref, gs_ref):
    t = ROUTE_T
    K = PEER_TOPK
    hb = h2_ref[...].astype(BF16)
    q = _dot(hb, wq_ref[...]).astype(BF16)
    for hj in range(2 * PEER_HEADS):
        q_ref[hj] = q[:, hj * PEER_HALF_DIM:(hj + 1) * PEER_HALF_DIM]

    key_ids = lax.broadcasted_iota(jnp.int32, (PEER_N_KEYS, t), 0).astype(F32)

    def level1(h, carry):
        scs = [_dot_nt(keys_ref[j], q_ref[2 * h + j]) for j in range(2)]
        res = _topk_rounds(scs, key_ids, K)
        for j in range(2):
            val_ref[2 * h + j] = res[j][0]
            idx_ref[2 * h + j] = res[j][1]
        return carry

    lax.fori_loop(0, PEER_HEADS, level1, 0)

    valid, pos_ids = _stair_tables(t)

    def level2(hp, carry):
        heads = [2 * hp, 2 * hp + 1]
        cands = [_stair_candidates(val_ref[2 * h], val_ref[2 * h + 1], idx_ref[2 * h], idx_ref[2 * h + 1], valid)
                 for h in heads]
        res = _topk_rounds([c[0] for c in cands], pos_ids, K, payloads=[c[1] for c in cands])
        for p, h in enumerate(heads):
            top_s = res[p][0]
            e = jnp.exp(top_s - top_s[0:1, :])
            gs_ref[h] = e / jnp.sum(e, axis=0, keepdims=True)
            es_ref[h] = res[p][2]
        return carry

    lax.fori_loop(0, PEER_HEADS // 2, level2, 0)

    eidx_ref[...] = es_ref[...].reshape(PEER_HEADS * K, t).T.astype(jnp.int32)
    gate_ref[...] = gs_ref[...].reshape(PEER_HEADS * K, t).T


def peer_route(h2, w_query, sub_keys):
    n, d = h2.shape
    t = ROUTE_T
    K = PEER_TOPK
    nsel = PEER_HEADS * K
    cst2 = lambda i: (0, 0)
    return pl.pallas_call(
        _route_kernel,
        grid=(n // t,),
        in_specs=[pl.BlockSpec((t, d), lambda i: (i, 0)),
                  pl.BlockSpec(w_query.shape, cst2),
                  pl.BlockSpec(sub_keys.shape, lambda i: (0, 0, 0))],
        out_specs=[pl.BlockSpec((t, nsel), lambda i: (i, 0)),
                   pl.BlockSpec((t, nsel), lambda i: (i, 0))],
        out_shape=[jax.ShapeDtypeStruct((n, nsel), jnp.int32),
                   jax.ShapeDtypeStruct((n, nsel), F32)],
        scratch_shapes=[pltpu.VMEM((2 * PEER_HEADS, t, PEER_HALF_DIM), BF16),
                        pltpu.VMEM((2 * PEER_HEADS, K, t), F32),
                        pltpu.VMEM((2 * PEER_HEADS, K, t), F32),
                        pltpu.VMEM((PEER_HEADS, K, t), F32),
                        pltpu.VMEM((PEER_HEADS, K, t), F32)],
        compiler_params=_cparams(("parallel",)),
        name="peer_route",
    )(h2, w_query, sub_keys)


SC_CORES = 2
SC_SUBCORES = 16
SC_WORKERS = SC_CORES * SC_SUBCORES
SC_LANES = 16
GATHER_ROWS = 32
GATHER_BUFS = 4
TOKEN_BLOCK = 8
COMBINE_COLS = 2
DOT_ROWS = 16


def pack_table(tab):
    e, d = tab.shape
    low = lax.bitcast_convert_type(tab[:, :d // 2].astype(BF16), jnp.uint16).astype(jnp.int32)
    hi_bits = lax.bitcast_convert_type(tab[:, d // 2:], jnp.uint32)
    sign = hi_bits & jnp.uint32(0x80000000)
    mag = (hi_bits & jnp.uint32(0x7FFFFFFF)).astype(jnp.int32)
    high = jnp.maximum(mag + 0x8000 - low, 0) >> 16
    word = sign | (high.astype(jnp.uint32) << 16) | low.astype(jnp.uint32)
    return lax.bitcast_convert_type(word, jnp.int32)


def _unpack_pair(w):
    lo = lax.bitcast_convert_type(jnp.left_shift(w, 16), F32)
    hi = lax.bitcast_convert_type(w, F32)
    return lo, hi


def _sc_mesh():
    return plsc.VectorSubcoreMesh(core_axis_name="c", subcore_axis_name="s")


def _worker_id():
    return lax.axis_index("s") * SC_CORES + lax.axis_index("c")


def _gather_pipeline(tab_hbm, idx_blk, rows_v, sems, n_chunks, chunks_per_tok, compute):
    ahead = GATHER_BUFS - 1

    def gather(q, buf):
        tok = q // chunks_per_tok
        ch = q % chunks_per_tok
        return pltpu.make_async_copy(
            tab_hbm.at[idx_blk.at[tok, pl.ds(ch * GATHER_ROWS, GATHER_ROWS)]],
            rows_v.at[buf], sems.at[buf])

    for q in range(ahead):
        gather(q, q).start()

    def ring(i, carry):
        for u in range(GATHER_BUFS):
            q = GATHER_BUFS * i + u

            @pl.when(q + ahead < n_chunks)
            def _():
                gather(q + ahead, (u + ahead) % GATHER_BUFS).start()

            gather(q, u).wait()
            compute(q, u)
        return carry

    lax.fori_loop(0, n_chunks // GATHER_BUFS, ring, 0)


def _block_pipeline(n_blocks, in_copies, out_copy, work):
    for cp in in_copies(0, 0):
        cp.start()

    def block(bi, carry):
        slot = bi % 2
        for cp in in_copies(bi, slot):
            cp.wait()

        @pl.when(bi + 1 < n_blocks)
        def _():
            for cp in in_copies(bi + 1, 1 - slot):
                cp.start()

        @pl.when(bi >= 2)
        def _():
            out_copy(bi - 2, slot).wait()

        work(bi, slot)
        out_copy(bi, slot).start()
        return carry

    lax.fori_loop(0, n_blocks, block, 0)
    out_copy(n_blocks - 2, n_blocks % 2).wait()
    out_copy(n_blocks - 1, (n_blocks - 1) % 2).wait()


def _dots_phase(u_hbm, h2_hbm, eidx_hbm, out_hbm, idx_v, x_v, rows_v, o_v, sems, in_sems, out_sems,
                n, d, nsel):
    tok_per_w = n // SC_WORKERS
    chunks_per_tok = nsel // GATHER_ROWS
    n_chunks = TOKEN_BLOCK * chunks_per_tok
    base = _worker_id() * tok_per_w

    def in_copies(bi, slot):
        t0 = base + bi * TOKEN_BLOCK
        return [pltpu.make_async_copy(eidx_hbm.at[pl.ds(t0, TOKEN_BLOCK)], idx_v.at[slot], in_sems.at[slot]),
                pltpu.make_async_copy(h2_hbm.at[pl.ds(t0, TOKEN_BLOCK)], x_v.at[slot], in_sems.at[slot])]

    def out_copy(bi, slot):
        t0 = base + bi * TOKEN_BLOCK
        return pltpu.make_async_copy(o_v.at[slot], out_hbm.at[pl.ds(t0, TOKEN_BLOCK)], out_sems.at[slot])

    def work(bi, slot):
        def compute(q, buf):
            tok = q // chunks_per_tok
            ch = q % chunks_per_tok
            zero = jnp.zeros((SC_LANES,), F32)
            for r0 in range(0, GATHER_ROWS, DOT_ROWS):
                def cbody(c, accs, r0=r0):
                    x_lo = x_v[slot, tok, pl.ds(c * SC_LANES, SC_LANES)]
                    x_hi = x_v[slot, tok, pl.ds(d // 2 + c * SC_LANES, SC_LANES)]
                    out = []
                    for r in range(DOT_ROWS):
                        lo, hi = _unpack_pair(rows_v[buf, r0 + r, pl.ds(c * SC_LANES, SC_LANES)])
                        out.append(accs[r] + lo * x_lo + hi * x_hi)
                    return tuple(out)

                accs = lax.fori_loop(0, d // (2 * SC_LANES), cbody, tuple(zero for _ in range(DOT_ROWS)))
                for r in range(DOT_ROWS):
                    o_v[slot, tok, pl.ds((ch * GATHER_ROWS + r0 + r) * SC_LANES, SC_LANES)] = accs[r]

        _gather_pipeline(u_hbm, idx_v.at[slot], rows_v, sems, n_chunks, chunks_per_tok, compute)

    _block_pipeline(tok_per_w // TOKEN_BLOCK, in_copies, out_copy, work)


def _combine_phase(v_hbm, w_hbm, eidx_hbm, out_hbm, idx_v, w_v, rows_v, y_v, sems, in_sems, out_sems,
                   n, d, nsel):
    tok_per_w = n // SC_WORKERS
    chunks_per_tok = nsel // GATHER_ROWS
    n_chunks = TOKEN_BLOCK * chunks_per_tok
    base = _worker_id() * tok_per_w

    def in_copies(bi, slot):
        t0 = base + bi * TOKEN_BLOCK
        return [pltpu.make_async_copy(eidx_hbm.at[pl.ds(t0, TOKEN_BLOCK)], idx_v.at[slot], in_sems.at[slot]),
                pltpu.make_async_copy(w_hbm.at[pl.ds(t0, TOKEN_BLOCK)], w_v.at[slot], in_sems.at[slot])]

    def out_copy(bi, slot):
        t0 = base + bi * TOKEN_BLOCK
        return pltpu.make_async_copy(y_v.at[slot], out_hbm.at[pl.ds(t0, TOKEN_BLOCK)], out_sems.at[slot])

    def work(bi, slot):
        slot_vec = jnp.full((SC_LANES,), slot, jnp.int32)

        def compute(q, buf):
            tok = q // chunks_per_tok
            ch = q % chunks_per_tok
            tok_vec = jnp.full((SC_LANES,), tok, jnp.int32)
            ws = [plsc.load_gather(
                w_v, [slot_vec, tok_vec, jnp.full((SC_LANES,), ch * GATHER_ROWS + r, jnp.int32)])
                for r in range(GATHER_ROWS)]
            first = ch == 0

            def cbody(cg, carry):
                cols = [cg * COMBINE_COLS + u for u in range(COMBINE_COLS)]
                sl_lo = [pl.ds(c * SC_LANES, SC_LANES) for c in cols]
                sl_hi = [pl.ds(d // 2 + c * SC_LANES, SC_LANES) for c in cols]
                zero = jnp.zeros((SC_LANES,), F32)
                a_lo = [jnp.where(first, zero, y_v[slot, tok, s]) for s in sl_lo]
                a_hi = [jnp.where(first, zero, y_v[slot, tok, s]) for s in sl_hi]
                for r in range(GATHER_ROWS):
                    for u in range(COMBINE_COLS):
                        lo, hi = _unpack_pair(rows_v[buf, r, pl.ds(cols[u] * SC_LANES, SC_LANES)])
                        a_lo[u] = a_lo[u] + ws[r] * lo
                        a_hi[u] = a_hi[u] + ws[r] * hi
                for u in range(COMBINE_COLS):
                    y_v[slot, tok, sl_lo[u]] = a_lo[u]
                    y_v[slot, tok, sl_hi[u]] = a_hi[u]
                return carry

            lax.fori_loop(0, d // (2 * SC_LANES) // COMBINE_COLS, cbody, 0)

        _gather_pipeline(v_hbm, idx_v.at[slot], rows_v, sems, n_chunks, chunks_per_tok, compute)

    _block_pipeline(tok_per_w // TOKEN_BLOCK, in_copies, out_copy, work)


def peer_experts_sc(u_pk=None, h2=None, eidx=None, v_pk=None, wts_prev=None, eidx_prev=None):
    do_dots = u_pk is not None
    do_comb = v_pk is not None
    n_c = wts_prev.shape[0] if do_comb else 0
    n_d = h2.shape[0] if do_dots else 0
    nsel = (eidx if do_dots else eidx_prev).shape[1]
    d = 2 * (u_pk if do_dots else v_pk).shape[1]
    out_types, scratch, args = [], [], []
    if do_comb:
        out_types.append(jax.ShapeDtypeStruct((n_c, d), F32))
        args += [v_pk, wts_prev, eidx_prev]
        scratch += [pltpu.VMEM((2, TOKEN_BLOCK, nsel), F32)]
    if do_dots:
        out_types.append(jax.ShapeDtypeStruct((n_d, nsel * SC_LANES), F32))
        args += [u_pk, h2, eidx]
        scratch += [pltpu.VMEM((2, TOKEN_BLOCK, nsel * SC_LANES), F32)]
    scratch += [pltpu.VMEM((2, TOKEN_BLOCK, d), F32),
                pltpu.VMEM((2, TOKEN_BLOCK, nsel), jnp.int32),
                pltpu.VMEM((GATHER_BUFS, GATHER_ROWS, d // 2), jnp.int32),
                pltpu.SemaphoreType.DMA((GATHER_BUFS,)),
                pltpu.SemaphoreType.DMA((2,)),
                pltpu.SemaphoreType.DMA((2,))]

    def body(*refs):
        refs = list(refs)
        ins = refs[:len(args)]
        outs = refs[len(args):len(args) + len(out_types)]
        scr = refs[len(args) + len(out_types):]
        xy_v, idx_v, rows_v, sems, in_sems, out_sems = scr[-6:]
        if do_comb:
            v_hbm, w_hbm, ep_hbm = ins[:3]
            _combine_phase(v_hbm, w_hbm, ep_hbm, outs[0], idx_v, scr[0], rows_v, xy_v,
                           sems, in_sems, out_sems, n_c, d, nsel)
        if do_dots:
            u_hbm, h2_hbm, e_hbm = ins[-3:]
            _dots_phase(u_hbm, h2_hbm, e_hbm, outs[-1], idx_v, xy_v, rows_v, scr[-7],
                        sems, in_sems, out_sems, n_d, d, nsel)

    res = pl.kernel(
        body, out_type=tuple(out_types), mesh=_sc_mesh(), scratch_types=scratch,
        compiler_params=pltpu.CompilerParams(needs_layout_passes=False),
        name="peer_experts_sc" + ("_c" if do_comb else "") + ("_d" if do_dots else ""),
    )(*args)
    res = list(res)
    y_prev = res[0] if do_comb else None
    part = res[-1] if do_dots else None
    return y_prev, part


def _weights_kernel(part_ref, gate_ref, red_ref, o_ref):
    act = jnp.dot(part_ref[...], red_ref[...], preferred_element_type=F32,
                  precision=lax.Precision.HIGHEST)
    gelu = 0.5 * act * (1.0 + lax.erf(act * (2.0 ** -0.5)))
    o_ref[...] = gate_ref[...] * gelu


def peer_weights(part2d, gate):
    n, nsel = gate.shape
    tm = 512
    w = part2d.shape[1]
    red = (jnp.arange(w)[:, None] // SC_LANES == jnp.arange(nsel)[None, :]).astype(F32)
    return pl.pallas_call(
        _weights_kernel,
        grid=(n // tm,),
        in_specs=[pl.BlockSpec((tm, w), lambda i: (i, 0)),
                  pl.BlockSpec((tm, nsel), lambda i: (i, 0)),
                  pl.BlockSpec((w, nsel), lambda i: (0, 0))],
        out_specs=pl.BlockSpec((tm, nsel), lambda i: (i, 0)),
        out_shape=jax.ShapeDtypeStruct((n, nsel), F32),
        compiler_params=_cparams(("parallel",)),
        name="peer_weights",
    )(part2d, gate, red)


def _final_kernel(x1_ref, y_ref, gate_ref, lg_ref, lb_ref, o_ref):
    o_ref[...] = _layer_norm(ALPHA * x1_ref[...] + gate_ref[0] * y_ref[...], lg_ref[...], lb_ref[...])


def final_ln(x1, y_ffn, gate2, ln_g, ln_b):
    n, d = x1.shape
    tm = 1024
    return pl.pallas_call(
        _final_kernel,
        grid=(n // tm,),
        in_specs=[pl.BlockSpec((tm, d), lambda i: (i, 0)),
                  pl.BlockSpec((tm, d), lambda i: (i, 0)),
                  pl.BlockSpec((1, 1, d), lambda i: (0, 0, 0)),
                  pl.BlockSpec((1, d), lambda i: (0, 0)),
                  pl.BlockSpec((1, d), lambda i: (0, 0))],
        out_specs=pl.BlockSpec((tm, d), lambda i: (i, 0)),
        out_shape=jax.ShapeDtypeStruct((n, d), F32),
        compiler_params=_cparams(("parallel",)),
        name="final_ln",
    )(x1, y_ffn, gate2, ln_g, ln_b)


def _cumsum(widths):
    out, t = [], 0
    for w in widths:
        t += w
        out.append(t)
    return out


def _block(x, c, w_ada, b_ada, w_in, conv_w, conv_b, dt_bias, a_log, d_skip, ssd_norm_w,
           lambda_q1, lambda_k1, lambda_q2, lambda_k2, da_subln_w, w_attn_branch, w_ssd_branch,
           w_out, ln1_g, ln1_b, peer_w_query, peer_sub_keys, peer_u, peer_v, ln2_g, ln2_b,
           layer):
    bsz, s, d = x.shape
    qk_w = DA_HEADS * 2 * DA_HEAD_DIM
    v_w = DA_HEADS * DA_V_DIM
    d_inner = w_ssd_branch.shape[0]
    n_heads = d_inner // SSD_HEAD_DIM
    bc_w = SSD_GROUPS * SSD_D_STATE
    xbc_w = d_inner + 2 * bc_w

    c_pad = jnp.pad(c, ((0, 8 - bsz), (0, 0)))
    mod = ada_mod(c_pad, w_ada, b_ada)[:bsz]
    shift1, scale1, gate1, shift2, scale2, gate2 = [m.reshape(bsz, 1, d) for m in jnp.split(mod, 6, axis=-1)]

    splits = _cumsum([qk_w, qk_w, v_w, d_inner, xbc_w, n_heads, 2 * d])
    wq, wk, wv, wz, wxbc, wdt, wg = [w_in[:, a:b] for a, b in zip([0] + splits[:-1], splits)]
    wq = wq * ((DA_HEAD_DIM ** -0.5) * LOG2E)
    w_main = jnp.concatenate([wxbc, wq, wz, wg, wk, wv], axis=1).astype(BF16)
    w_dt = jnp.pad(wdt, ((0, 0), (0, 128 - n_heads))).astype(BF16)
    xbc_c0 = 0
    q_c0 = xbc_w
    z_c0 = q_c0 + qk_w
    g_c0 = z_c0 + d_inner
    k_c0 = g_c0 + 2 * d
    v_c0 = k_c0 + qk_w
    lambda_init = 0.8 - 0.6 * math.exp(-0.3 * layer)
    lam_params = jnp.stack([lambda_q1, lambda_k1, lambda_q2, lambda_k2]).astype(F32)
    pad_h = lambda t: jnp.pad(t.astype(F32), (0, 128 - n_heads)).reshape(1, 128)
    dtb_p, alog_p = pad_h(dt_bias), pad_h(a_log)
    dskip_e = jnp.repeat(d_skip.astype(F32), SSD_HEAD_DIM).reshape(1, d_inner)
    w_ab, w_sb, w_o = w_attn_branch.astype(BF16), w_ssd_branch.astype(BF16), w_out.astype(BF16)
    w_qp, keys_b = peer_w_query.astype(BF16), peer_sub_keys.astype(BF16)
    u_pk, v_pk = pack_table(peer_u), pack_table(peer_v)

    outs = []
    pending = None

    def retire(cur):
        nonlocal pending
        kw = {}
        if pending is not None:
            p_x1, p_gate2, p_eidx, p_gate, p_part = pending
            wts = peer_weights(p_part, p_gate)
            kw.update(v_pk=v_pk, wts_prev=wts, eidx_prev=p_eidx)
        if cur is not None:
            kw.update(u_pk=u_pk, h2=cur[4], eidx=cur[2])
        y_prev, part = peer_experts_sc(**kw)
        if pending is not None:
            outs.append(final_ln(p_x1, y_prev, p_gate2, ln2_g.reshape(1, d), ln2_b.reshape(1, d)))
        pending = None if cur is None else (cur[0], cur[1], cur[2], cur[3], part)

    for b in range(bsz):
        sl = slice(b, b + 1)
        xb = x[b]
        proj, dt_raw = in_proj(xb, shift1[sl], scale1[sl], w_main, w_dt, s)
        proj3 = proj.reshape(1, s, proj.shape[1])
        y_ssd = ssd_branch(proj3, dt_raw.reshape(1, s, 128), conv_w, conv_b.reshape(1, xbc_w),
                           dtb_p, alog_p, dskip_e, ssd_norm_w.reshape(1, d_inner),
                           xbc_c0, z_c0, d_inner, n_heads).reshape(s, d_inner)
        row0 = 0
        for gs in _group_sizes(s, first=(b == 0), last=(b == bsz - 1)):
            y_attn = diff_attention(proj3, lam_params, da_subln_w.reshape(1, DA_V_DIM), q_c0, k_c0, v_c0,
                                    lambda_init, row0 // ATT_TQ, gs // ATT_TQ)
            x1, h2 = merge_ln1(y_attn.reshape(gs, v_w), y_ssd, proj, xb,
                               gate1[sl], shift2[sl], scale2[sl], w_ab, w_sb, w_o,
                               ln1_g.reshape(1, d), ln1_b.reshape(1, d), g_c0, row0)
            eidx, gate = peer_route(h2, w_qp, keys_b)
            retire((x1, gate2[sl], eidx, gate, h2))
            row0 += gs
    retire(None)
    return jnp.concatenate(outs, axis=0).reshape(bsz, s, d)


def _group_sizes(s, first, last):
    e = s // 8
    if first and last:
        return [e, 3 * e, 2 * e, e, e]
    if first:
        return [e, 3 * e, 4 * e]
    if last:
        return [4 * e, 2 * e, e, e]
    return [4 * e, 4 * e]


def kernel(x, c, w_ada, b_ada, w_in, conv_w, conv_b, dt_bias, a_log, d_skip, ssd_norm_w,
           lambda_q1, lambda_k1, lambda_q2, lambda_k2, da_subln_w, w_attn_branch, w_ssd_branch,
           w_out, ln1_g, ln1_b, peer_w_query, peer_sub_keys, peer_u, peer_v, ln2_g, ln2_b):
    for l in range(w_ada.shape[0]):
        x = _block(x, c, w_ada[l], b_ada[l], w_in[l], conv_w[l], conv_b[l], dt_bias[l], a_log[l],
                   d_skip[l], ssd_norm_w[l], lambda_q1[l], lambda_k1[l], lambda_q2[l], lambda_k2[l],
                   da_subln_w[l], w_attn_branch[l], w_ssd_branch[l], w_out[l], ln1_g[l], ln1_b[l],
                   peer_w_query[l], peer_sub_keys[l], peer_u[l], peer_v[l], ln2_g[l], ln2_b[l], l)
    return x
```
